```python
import math
import jax, jax.numpy as jnp
from jax import lax
import numpy as np

D_MODEL = 1024
BATCH = 8
SEQ = 4096
DEPTH = 2

EPS = 1e-6
BLOCK_Q = 128
D_FF = 2816
GLA_HEADS = 4
GLA_DK = 64
GLA_DV = 128
GLA_GATE_RANK = 16
GLA_GATE_NORMALIZER = 16.0
GLA_CHUNK = 64
FOX_HEADS = 8
FOX_DH = 64
DSA_HEADS = 16
DSA_DH = 64
DSA_LATENT = 256
IDX_HEADS = 8
IDX_DIM = 64
TOPK_MAX = 256
T5_BUCKETS = 32
T5_MAX_DIST = 128

N_EVEN = (DEPTH + 1) // 2
N_ODD = DEPTH // 2

EVEN_SIZES = (GLA_HEADS * GLA_DK, GLA_HEADS * GLA_DK, GLA_HEADS * GLA_DV, GLA_HEADS * GLA_DV, GLA_GATE_RANK,
              FOX_HEADS * FOX_DH, FOX_HEADS * FOX_DH, FOX_HEADS * FOX_DH, FOX_HEADS)
ODD_SIZES = (DSA_HEADS * DSA_DH, DSA_LATENT, IDX_HEADS * IDX_DIM, IDX_DIM, IDX_HEADS)
D_IN_EVEN = sum(EVEN_SIZES)
D_IN_ODD = sum(ODD_SIZES)
EVEN_SPLITS = tuple(int(v) for v in np.cumsum(EVEN_SIZES)[:-1])
ODD_SPLITS = tuple(int(v) for v in np.cumsum(ODD_SIZES)[:-1])
D_MIX_EVEN = GLA_HEADS * GLA_DV + FOX_HEADS * FOX_DH
D_MIX_ODD = DSA_HEADS * DSA_DH

kernel_name = "hybrid_gla_fox_dsa_macaron"


def rmsnorm(x, g):
    xf = x.astype(jnp.float32)
    y = xf * lax.rsqrt(jnp.mean(xf * xf, axis=-1, keepdims=True) + EPS)
    return (y * g.astype(jnp.float32)).astype(x.dtype)


def swiglu(h, w_in, w_out):
    a, b = jnp.split(h @ w_in, 2, axis=-1)
    return (jax.nn.silu(a) * b) @ w_out


def to_blocks(t, nb):
    return t.reshape((t.shape[0], nb, BLOCK_Q) + t.shape[2:]).swapaxes(0, 1)


def gla_chunked(q, k, v, gk):
    B, S, H, dk = q.shape
    dv = v.shape[-1]
    C = GLA_CHUNK
    N = S // C

    def chunks(t):
        return t.astype(jnp.float32).reshape(B, N, C, H, t.shape[-1]).transpose(1, 0, 3, 2, 4)

    qc = chunks(q) * (dk ** -0.5)
    kc, vc, gc = chunks(k), chunks(v), chunks(gk)
    G = jnp.cumsum(gc, axis=-2)
    G_last = G[..., -1, :]
    q_dec = qc * jnp.exp(G)
    k_dec = kc * jnp.exp(-G)
    k_to_end = kc * jnp.exp(G_last[..., None, :] - G)
    causal = jnp.tril(jnp.ones((C, C), dtype=bool))
    A = jnp.where(causal, jnp.einsum('nbhid,nbhjd->nbhij', q_dec, k_dec), 0.0)
    o_intra = jnp.einsum('nbhij,nbhjv->nbhiv', A, vc)

    def step(state, inp):
        qd, ke, vv, gl = inp
        o = jnp.einsum('bhid,bhdv->bhiv', qd, state)
        state = jnp.exp(gl)[..., None] * state + jnp.einsum('bhjd,bhjv->bhdv', ke, vv)
        return state, o

    state0 = jnp.zeros((B, H, dk, dv), jnp.float32)
    _, o_inter = lax.scan(step, state0, (q_dec, k_to_end, vc, G_last))
    o = o_intra + o_inter
    return o.transpose(1, 0, 3, 2, 4).reshape(B, S, H, dv)


def fox_attention(q, k, v, log_f):
    B, S, H, dh = q.shape
    nb = S // BLOCK_Q
    F = jnp.cumsum(log_f, axis=1)
    F_keys = F.transpose(0, 2, 1)
    key_pos = jnp.arange(S)

    def one_block(args):
        i, qb, Fb = args
        q_pos = i * BLOCK_Q + jnp.arange(BLOCK_Q)
        s = jnp.einsum('bqhd,bkhd->bhqk', qb, k).astype(jnp.float32) * (dh ** -0.5)
        s = s + (Fb.transpose(0, 2, 1)[..., :, None] - F_keys[..., None, :])
        s = jnp.where(key_pos[None, :] <= q_pos[:, None], s, -jnp.inf)
        p = jax.nn.softmax(s, axis=-1)
        return jnp.einsum('bhqk,bkhd->bqhd', p.astype(v.dtype), v)

    o = lax.map(one_block, (jnp.arange(nb), to_blocks(q, nb), to_blocks(F, nb)))
    return o.swapaxes(0, 1).reshape(B, S, H, dh)


def t5_bucket(dist):
    max_exact = T5_BUCKETS // 2
    d = jnp.maximum(dist, 1).astype(jnp.float32)
    large = max_exact + (jnp.log(d / max_exact) / math.log(T5_MAX_DIST / max_exact)
                         * (T5_BUCKETS - max_exact)).astype(jnp.int32)
    large = jnp.minimum(large, T5_BUCKETS - 1)
    return jnp.where(dist < max_exact, dist, large)


def dsa_attention(q_lat, c, q_idx, k_idx, w_idx, t5_table):
    B, S, H, DL = q_lat.shape
    k_top = min(TOPK_MAX, S // 4)
    nb = S // BLOCK_Q
    key_pos = jnp.arange(S)
    w_scaled = w_idx.astype(jnp.float32) * (IDX_HEADS ** -0.5)
    gather = jax.vmap(lambda cb, ib: cb[ib])

    def one_block(args):
        i, qb, qib, wb = args
        q_pos = i * BLOCK_Q + jnp.arange(BLOCK_Q)
        logit_idx = jnp.einsum('bqhd,bkd->bqhk', qib, k_idx).astype(jnp.float32) * (IDX_DIM ** -0.5)
        score = jnp.einsum('bqh,bqhk->bqk', wb, jax.nn.relu(logit_idx))
        score = jnp.where((key_pos[None, :] <= q_pos[:, None])[None], score, -jnp.inf)
        _, idx = lax.top_k(score, k_top)
        c_sel = gather(c, idx)
        dist = q_pos[None, :, None] - idx
        bias = t5_table[t5_bucket(jnp.maximum(dist, 0))]
        s = jnp.einsum('bqhl,bqkl->bqhk', qb, c_sel).astype(jnp.float32) * (DSA_DH ** -0.5)
        s = s + bias.transpose(0, 1, 3, 2).astype(jnp.float32)
        s = jnp.where((dist >= 0)[:, :, None, :], s, -jnp.inf)
        p = jax.nn.softmax(s, axis=-1)
        return jnp.einsum('bqhk,bqkl->bqhl', p.astype(c.dtype), c_sel)

    o = lax.map(one_block, (jnp.arange(nb), to_blocks(q_lat, nb), to_blocks(q_idx, nb), to_blocks(w_scaled, nb)))
    return o.swapaxes(0, 1).reshape(B, S, H, DL)


def even_mixer(h, w_in, gla_w_gate, gla_b_gate, gla_norm_g, fox_b_f, w_out):
    B, S, _ = h.shape
    q_g, k_g, v_g, g_out, gk_low, q_f, k_f, v_f, f_logit = jnp.split(h @ w_in, EVEN_SPLITS, axis=-1)
    gk = jax.nn.log_sigmoid((gk_low @ gla_w_gate + gla_b_gate).astype(jnp.float32)) / GLA_GATE_NORMALIZER
    o_gla = gla_chunked(q_g.reshape(B, S, GLA_HEADS, GLA_DK), k_g.reshape(B, S, GLA_HEADS, GLA_DK),
                        v_g.reshape(B, S, GLA_HEADS, GLA_DV), gk.reshape(B, S, GLA_HEADS, GLA_DK)).astype(h.dtype)
    o_gla = rmsnorm(o_gla, gla_norm_g) * jax.nn.silu(g_out.reshape(B, S, GLA_HEADS, GLA_DV))
    log_f = jax.nn.log_sigmoid((f_logit + fox_b_f).astype(jnp.float32))
    o_fox = fox_attention(q_f.reshape(B, S, FOX_HEADS, FOX_DH), k_f.reshape(B, S, FOX_HEADS, FOX_DH),
                          v_f.reshape(B, S, FOX_HEADS, FOX_DH), log_f)
    o = jnp.concatenate([o_gla.reshape(B, S, -1), o_fox.reshape(B, S, -1)], axis=-1)
    return o @ w_out


def odd_mixer(h, w_in, kv_norm_g, w_uk, w_uv, w_out, t5_table):
    B, S, _ = h.shape
    q, ckv, q_idx, k_idx, w_idx = jnp.split(h @ w_in, ODD_SPLITS, axis=-1)
    c = rmsnorm(ckv, kv_norm_g)
    q_lat = jnp.einsum('bshd,hdl->bshl', q.reshape(B, S, DSA_HEADS, DSA_DH), w_uk)
    o_lat = dsa_attention(q_lat, c, q_idx.reshape(B, S, IDX_HEADS, IDX_DIM), k_idx, w_idx, t5_table)
    o = jnp.einsum('bshl,hld->bshd', o_lat, w_uv).reshape(B, S, D_MIX_ODD)
    return o @ w_out


def setup_inputs(seed: int = 0) -> dict:
    key = jax.random.key(seed)
    ks = jax.random.split(key, 17)
    f32 = jnp.float32

    def nrm(k, shape, fan_in):
        return jax.random.normal(k, shape, f32) * (fan_in ** -0.5)

    def gain(k, shape):
        return 1.0 + 0.02 * jax.random.normal(k, shape, f32)

    return {
        "x": jax.random.normal(ks[0], (BATCH, SEQ, D_MODEL), f32),
        "norm_g": gain(ks[1], (DEPTH, 3, D_MODEL)),
        "ffn_w_in": nrm(ks[2], (DEPTH, 2, D_MODEL, 2 * D_FF), D_MODEL),
        "ffn_w_out": nrm(ks[3], (DEPTH, 2, D_FF, D_MODEL), D_FF),
        "even_w_in": nrm(ks[4], (N_EVEN, D_MODEL, D_IN_EVEN), D_MODEL),
        "gla_w_gate": nrm(ks[5], (N_EVEN, GLA_GATE_RANK, GLA_HEADS * GLA_DK), GLA_GATE_RANK),
        "gla_b_gate": 0.02 * jax.random.normal(ks[6], (N_EVEN, GLA_HEADS * GLA_DK), f32),
        "gla_norm_g": gain(ks[7], (N_EVEN, GLA_DV)),
        "fox_b_f": 2.0 + 0.5 * jax.random.normal(ks[8], (N_EVEN, FOX_HEADS), f32),
        "even_w_out": nrm(ks[9], (N_EVEN, D_MIX_EVEN, D_MODEL), D_MIX_EVEN),
        "odd_w_in": nrm(ks[10], (N_ODD, D_MODEL, D_IN_ODD), D_MODEL),
        "mla_kv_norm_g": gain(ks[11], (N_ODD, DSA_LATENT)),
        "mla_w_uk": nrm(ks[12], (N_ODD, DSA_HEADS, DSA_DH, DSA_LATENT), DSA_LATENT),
        "mla_w_uv": nrm(ks[13], (N_ODD, DSA_HEADS, DSA_LATENT, DSA_DH), DSA_LATENT),
        "odd_w_out": nrm(ks[14], (N_ODD, D_MIX_ODD, D_MODEL), D_MIX_ODD),
        "t5_table": 0.5 * jax.random.normal(ks[15], (T5_BUCKETS, DSA_HEADS), f32),
        "final_norm_g": gain(ks[16], (D_MODEL,)),
    }


def reference(x, norm_g, ffn_w_in, ffn_w_out, even_w_in, gla_w_gate, gla_b_gate, gla_norm_g, fox_b_f,
              even_w_out, odd_w_in, mla_kv_norm_g, mla_w_uk, mla_w_uv, odd_w_out, t5_table, final_norm_g):
    for layer in range(DEPTH):
        g = norm_g[layer]
        x = x + 0.5 * swiglu(rmsnorm(x, g[0]), ffn_w_in[layer, 0], ffn_w_out[layer, 0])
        h = rmsnorm(x, g[1])
        j = layer // 2
        if layer % 2 == 0:
            x = x + even_mixer(h, even_w_in[j], gla_w_gate[j], gla_b_gate[j], gla_norm_g[j], fox_b_f[j], even_w_out[j])
        else:
            x = x + odd_mixer(h, odd_w_in[j], mla_kv_norm_g[j], mla_w_uk[j], mla_w_uv[j], odd_w_out[j], t5_table)
        x = x + 0.5 * swiglu(rmsnorm(x, g[2]), ffn_w_in[layer, 1], ffn_w_out[layer, 1])
    return rmsnorm(x, final_norm_g)
```

```python
import functools
import math

import numpy as np
import jax
import jax.numpy as jnp
from jax import lax
from jax.experimental import pallas as pl
from jax.experimental.pallas import tpu as pltpu

F32 = jnp.float32
BF16 = jnp.bfloat16

EPS = 1e-6
GLA_HEADS = 4
GLA_DK = 64
GLA_DV = 128
GLA_GATE_RANK = 16
GLA_GATE_NORMALIZER = 16.0
GLA_CHUNK = 64
FOX_HEADS = 8
FOX_DH = 64
DSA_HEADS = 16
DSA_DH = 64
DSA_LATENT = 256
IDX_HEADS = 8
IDX_DIM = 64
TOPK_MAX = 256
T5_BUCKETS = 32
T5_MAX_DIST = 128

LANES = 128
VMEM_LIMIT = 56 * 1024 * 1024

ROW_TILE = 512
FFN_CHUNK = 256
GLA_ROWS = 256
FOX_TQ = 256
DSA_TQ = 128
DSA_TK = 256
NEG_BIG = -1e30
INT_MIN = -2 ** 31
KEY_NEG_INF = INT_MIN + 0x7FFFFF

_CONTRACT_LAST = (((1,), (1,)), ((), ()))
_CONTRACT_FIRST = (((0,), (0,)), ((), ()))


def _params(*sem):
    return pltpu.CompilerParams(dimension_semantics=sem, vmem_limit_bytes=VMEM_LIMIT)


def _resident(shape):
    nd = len(shape)
    return pl.BlockSpec(shape, lambda *_: (0,) * nd, pipeline_mode=pl.Buffered(1))


def _rms(x, g):
    return x * lax.rsqrt(jnp.mean(x * x, axis=-1, keepdims=True) + EPS) * g


def _log_sigmoid(x):
    return jnp.minimum(x, 0.0) - jnp.log1p(jnp.exp(-jnp.abs(x)))


def _silu(x):
    return x * jax.nn.sigmoid(x)


def _ffn_kernel(*refs, n_chunks, final):
    if final:
        x_ref, g_ref, wa_ref, wb_ref, wo_ref, gf_ref, o_ref, h_ref, acc_ref = refs
    else:
        x_ref, g_ref, wa_ref, wb_ref, wo_ref, o_ref, h_ref, acc_ref = refs
    x = x_ref[...]
    h_ref[...] = _rms(x, g_ref[...]).astype(BF16)
    acc_ref[...] = jnp.zeros_like(acc_ref)

    def body(j, carry):
        h = h_ref[...]
        a = jnp.dot(h, wa_ref[j], preferred_element_type=F32)
        b = jnp.dot(h, wb_ref[j], preferred_element_type=F32)
        act = (_silu(a) * b).astype(BF16)
        acc_ref[...] += jnp.dot(act, wo_ref[j], preferred_element_type=F32)
        return carry

    lax.fori_loop(0, n_chunks, body, 0)
    y = x + 0.5 * acc_ref[...]
    if final:
        y = _rms(y, gf_ref[...])
    o_ref[...] = y


def _ffn(x2, g, w_in, w_out, final_g=None):
    t, d = x2.shape
    f = w_out.shape[0]
    n_chunks = f // FFN_CHUNK
    wa = w_in[:, :f].astype(BF16).reshape(d, n_chunks, FFN_CHUNK).transpose(1, 0, 2)
    wb = w_in[:, f:].astype(BF16).reshape(d, n_chunks, FFN_CHUNK).transpose(1, 0, 2)
    wo = w_out.astype(BF16).reshape(n_chunks, FFN_CHUNK, d)
    final = final_g is not None
    row = pl.BlockSpec((ROW_TILE, d), lambda i: (i, 0))
    in_specs = [row, _resident((1, d)), _resident(wa.shape), _resident(wb.shape), _resident(wo.shape)]
    args = [x2, g.reshape(1, d), wa, wb, wo]
    if final:
        in_specs.append(_resident((1, d)))
        args.append(final_g.reshape(1, d))
    return pl.pallas_call(
        functools.partial(_ffn_kernel, n_chunks=n_chunks, final=final),
        grid=(t // ROW_TILE,),
        in_specs=in_specs,
        out_specs=row,
        out_shape=jax.ShapeDtypeStruct((t, d), F32),
        scratch_shapes=[pltpu.VMEM((ROW_TILE, d), BF16), pltpu.VMEM((ROW_TILE, d), F32)],
        compiler_params=_params("parallel"),
        name="ffn",
    )(*args)


def _inproj_kernel(*refs, big_chunks, with_c):
    if with_c:
        x_ref, g_ref, wbig_ref, wsm_ref, wc_ref, gc_ref, big_ref, sm_ref, c_ref = refs
    else:
        x_ref, g_ref, wbig_ref, wsm_ref, big_ref, sm_ref = refs
    h = _rms(x_ref[...], g_ref[...]).astype(BF16)
    for lo, hi in big_chunks:
        big_ref[:, lo:hi] = jnp.dot(h, wbig_ref[:, lo:hi], preferred_element_type=F32).astype(BF16)
    sm_ref[...] = jnp.dot(h, wsm_ref[...], preferred_element_type=F32)
    if with_c:
        ckv = jnp.dot(h, wc_ref[...], preferred_element_type=F32)
        c_ref[...] = _rms(ckv, gc_ref[...]).astype(BF16)


def _inproj(x2, g, w_big, w_small, w_c=None, g_c=None):
    t, d = x2.shape
    nbig = w_big.shape[1]
    with_c = w_c is not None
    step = 4 * LANES
    big_chunks = tuple((lo, min(lo + step, nbig)) for lo in range(0, nbig, step))
    row = lambda n: pl.BlockSpec((ROW_TILE, n), lambda i: (i, 0))
    in_specs = [row(d), _resident((1, d)), _resident(w_big.shape), _resident(w_small.shape)]
    args = [x2, g.reshape(1, d), w_big.astype(BF16), w_small.astype(BF16)]
    out_specs = [row(nbig), row(LANES)]
    out_shape = [jax.ShapeDtypeStruct((t, nbig), BF16), jax.ShapeDtypeStruct((t, LANES), F32)]
    if with_c:
        nc = w_c.shape[1]
        in_specs += [_resident(w_c.shape), _resident((1, nc))]
        args += [w_c.astype(BF16), g_c.reshape(1, nc)]
        out_specs.append(row(nc))
        out_shape.append(jax.ShapeDtypeStruct((t, nc), BF16))
    return pl.pallas_call(
        functools.partial(_inproj_kernel, big_chunks=big_chunks, with_c=with_c),
        grid=(t // ROW_TILE,),
        in_specs=in_specs,
        out_specs=out_specs,
        out_shape=out_shape,
        compiler_params=_params("parallel"),
        name="inproj_c" if with_c else "inproj",
    )(*args)


def _outproj_kernel(*refs, n_in):
    x_ref = refs[0]
    o_refs = refs[1:1 + n_in]
    w_refs = refs[1 + n_in:1 + 2 * n_in]
    y_ref = refs[1 + 2 * n_in]
    y = x_ref[...]
    for o_ref, w_ref in zip(o_refs, w_refs):
        y = y + jnp.dot(o_ref[...], w_ref[...], preferred_element_type=F32)
    y_ref[...] = y


def _outproj(x2, outs, w_out):
    t, d = x2.shape
    ws, lo = [], 0
    for o in outs:
        ws.append(w_out[lo:lo + o.shape[1]].astype(BF16))
        lo += o.shape[1]
    row = lambda n: pl.BlockSpec((ROW_TILE, n), lambda i: (i, 0))
    return pl.pallas_call(
        functools.partial(_outproj_kernel, n_in=len(outs)),
        grid=(t // ROW_TILE,),
        in_specs=[row(d)] + [row(o.shape[1]) for o in outs] + [_resident(w.shape) for w in ws],
        out_specs=row(d),
        out_shape=jax.ShapeDtypeStruct((t, d), F32),
        compiler_params=_params("parallel"),
        name="outproj",
    )(x2, *outs, *ws)


def _gla_kernel(q_ref, k_ref, v_ref, go_ref, sm_ref, wg_ref, bg_ref, ng_ref, o_ref, st_ref):
    rs = q_ref.shape[1]
    c_len = GLA_CHUNK

    @pl.when(pl.program_id(1) == 0)
    def _():
        st_ref[...] = jnp.zeros_like(st_ref)

    pre = jnp.dot(sm_ref[0].astype(BF16), wg_ref[...], preferred_element_type=F32) + bg_ref[...]
    gk = _log_sigmoid(pre) / GLA_GATE_NORMALIZER
    r = lax.broadcasted_iota(jnp.int32, (rs, rs), 0)
    c = lax.broadcasted_iota(jnp.int32, (rs, rs), 1)
    same = (r // c_len) == (c // c_len)
    tri = jnp.where(same & (c <= r), 1.0, 0.0).astype(F32)
    ones = jnp.where(same, 1.0, 0.0).astype(F32)
    g_cum = jnp.dot(tri, gk, precision=lax.Precision.HIGHEST, preferred_element_type=F32)
    g_last = jnp.dot(ones, gk, precision=lax.Precision.HIGHEST, preferred_element_type=F32)
    q = q_ref[0].astype(F32) * (GLA_DK ** -0.5)
    k = k_ref[0].astype(F32)
    q_dec = (q * jnp.exp(g_cum)).astype(BF16)
    k_dec = (k * jnp.exp(-g_cum)).astype(BF16)
    k_end = (k * jnp.exp(g_last - g_cum)).astype(BF16)
    decay = jnp.exp(g_last)

    lane = lax.broadcasted_iota(jnp.int32, (1, LANES), 1)
    ri = lax.broadcasted_iota(jnp.int32, (c_len, c_len), 0)
    ci = lax.broadcasted_iota(jnp.int32, (c_len, c_len), 1)
    causal = ci <= ri
    ng = ng_ref[...]
    zero = jnp.zeros((), BF16)
    for ch in range(rs // c_len):
        rows = slice(ch * c_len, (ch + 1) * c_len)
        for h in range(GLA_HEADS):
            pair = slice((h // 2) * LANES, (h // 2 + 1) * LANES)
            mine = (lane // GLA_DK) == (h % 2)
            vcol = slice(h * GLA_DV, (h + 1) * GLA_DV)
            qm = jnp.where(mine, q_dec[rows, pair], zero)
            a = lax.dot_general(qm, k_dec[rows, pair], _CONTRACT_LAST, preferred_element_type=F32)
            a = jnp.where(causal, a, 0.0)
            vh = v_ref[0, rows, vcol]
            st = st_ref[h]
            o = jnp.dot(a.astype(BF16), vh, preferred_element_type=F32)
            o = o + lax.dot_general(qm, st.astype(BF16), _CONTRACT_LAST, preferred_element_type=F32)
            km = jnp.where(mine, k_end[rows, pair], zero)
            upd = lax.dot_general(vh, km, _CONTRACT_FIRST, preferred_element_type=F32)
            st_ref[h] = st * decay[ch * c_len:ch * c_len + 1, pair] + upd
            on = _rms(o, ng)
            gate = go_ref[0, rows, vcol].astype(F32)
            o_ref[0, rows, vcol] = (on * _silu(gate)).astype(BF16)


def _gla(big, small, w_gate, b_gate, norm_g):
    b, s, _ = big.shape
    hk = GLA_HEADS * GLA_DK
    hv = GLA_HEADS * GLA_DV
    wg = jnp.zeros((LANES, hk), F32).at[:GLA_GATE_RANK].set(w_gate).astype(BF16)
    blk = lambda n, cb: pl.BlockSpec((1, GLA_ROWS, n), lambda bi, si: (bi, si, cb))
    return pl.pallas_call(
        _gla_kernel,
        grid=(b, s // GLA_ROWS),
        in_specs=[blk(hk, 0), blk(hk, 1), blk(hv, 1), blk(hv, 2), blk(LANES, 0),
                  _resident(wg.shape), _resident((1, hk)), _resident((1, GLA_DV))],
        out_specs=blk(hv, 0),
        out_shape=jax.ShapeDtypeStruct((b, s, hv), BF16),
        scratch_shapes=[pltpu.VMEM((GLA_HEADS, GLA_DV, LANES), F32)],
        compiler_params=_params("parallel", "arbitrary"),
        name="gla",
    )(big, big, big, big, small, wg, b_gate.reshape(1, hk), norm_g.reshape(1, GLA_DV))


def _foxgate_kernel(fl_ref, b_ref, f_ref):
    x = _log_sigmoid(fl_ref[0] + b_ref[...])
    s = x.shape[1]
    lane = lax.broadcasted_iota(jnp.int32, x.shape, 1)
    sh = 1
    while sh < s:
        x = x + jnp.where(lane >= sh, pltpu.roll(x, sh, 1), 0.0)
        sh *= 2
    f_ref[0] = x


def _fox_gate(f_logit_t, bias):
    b, h, s = f_logit_t.shape
    blk = pl.BlockSpec((1, h, s), lambda bi: (bi, 0, 0))
    return pl.pallas_call(
        _foxgate_kernel,
        grid=(b,),
        in_specs=[blk, _resident((h, 1))],
        out_specs=blk,
        out_shape=jax.ShapeDtypeStruct((b, h, s), F32),
        compiler_params=_params("parallel"),
        name="fox_gate",
    )(f_logit_t, bias.reshape(h, 1))


def _fox_kernel(q_ref, k_ref, v_ref, f_ref, o_ref, m_ref, l_ref, acc_ref):
    tq = q_ref.shape[1]
    tk = tq
    pr = pl.program_id(1)
    qi = pl.program_id(2)
    lane = lax.broadcasted_iota(jnp.int32, (1, LANES), 1)
    q = q_ref[0] * jnp.asarray(FOX_DH ** -0.5, BF16)
    zero = jnp.zeros((), BF16)
    qm = [jnp.where((lane // FOX_DH) == hh, q, zero) for hh in range(2)]
    m_ref[...] = jnp.full_like(m_ref, NEG_BIG)
    l_ref[...] = jnp.zeros_like(l_ref)
    acc_ref[...] = jnp.zeros_like(acc_ref)
    row = lax.broadcasted_iota(jnp.int32, (tq, tk), 0)
    col = lax.broadcasted_iota(jnp.int32, (tq, tk), 1)

    def block(j, diagonal):
        start = pl.multiple_of(j * tk, tk)
        kb = k_ref[0, pl.ds(start, tk), :]
        vb = v_ref[0, pl.ds(start, tk), :]
        for hh in range(2):
            fk = f_ref[0, 2 * pr + hh, pl.ds(j, 1), :]
            s = lax.dot_general(qm[hh], kb, _CONTRACT_LAST, preferred_element_type=F32) - fk
            if diagonal:
                s = jnp.where(col <= row, s, NEG_BIG)
            m_old = m_ref[hh]
            m_new = jnp.maximum(m_old, jnp.max(s, axis=-1, keepdims=True))
            alpha = jnp.exp(m_old - m_new)
            p = jnp.exp(s - m_new)
            l_ref[hh] = alpha * l_ref[hh] + jnp.sum(p, axis=-1, keepdims=True)
            acc_ref[hh] = alpha * acc_ref[hh] + jnp.dot(p.astype(BF16), vb, preferred_element_type=F32)
            m_ref[hh] = m_new

    def off_diag(j, carry):
        block(j, False)
        return carry

    lax.fori_loop(0, qi, off_diag, 0)
    block(qi, True)
    o = jnp.where((lane // FOX_DH) == 0, acc_ref[0] / l_ref[0], acc_ref[1] / l_ref[1])
    o_ref[0] = o.astype(BF16)


def _fox(big, f_cum, q_col, k_col, v_col):
    b, s, _ = big.shape
    n_pairs = FOX_HEADS // 2
    nkb = s // FOX_TQ
    f4 = f_cum.reshape(b, FOX_HEADS, nkb, FOX_TQ)
    return pl.pallas_call(
        _fox_kernel,
        grid=(b, n_pairs, s // FOX_TQ),
        in_specs=[
            pl.BlockSpec((1, FOX_TQ, LANES), lambda bi, pi, qi: (bi, qi, q_col + pi)),
            pl.BlockSpec((1, s, LANES), lambda bi, pi, qi: (bi, 0, k_col + pi)),
            pl.BlockSpec((1, s, LANES), lambda bi, pi, qi: (bi, 0, v_col + pi)),
            pl.BlockSpec((1, FOX_HEADS, nkb, FOX_TQ), lambda bi, pi, qi: (bi, 0, 0, 0)),
        ],
        out_specs=pl.BlockSpec((1, FOX_TQ, LANES), lambda bi, pi, qi: (bi, qi, pi)),
        out_shape=jax.ShapeDtypeStruct((b, s, FOX_HEADS * FOX_DH), BF16),
        scratch_shapes=[pltpu.VMEM((2, FOX_TQ, 1), F32), pltpu.VMEM((2, FOX_TQ, 1), F32),
                        pltpu.VMEM((2, FOX_TQ, LANES), F32)],
        compiler_params=_params("parallel", "parallel", "arbitrary"),
        name="fox",
    )(big, big, big, f4)


def _t5_bucket_np(dist):
    max_exact = T5_BUCKETS // 2
    d = np.maximum(dist, 1).astype(np.float32)
    large = max_exact + (np.log(d / np.float32(max_exact)) / np.float32(math.log(T5_MAX_DIST / max_exact))
                         * np.float32(T5_BUCKETS - max_exact)).astype(np.int32)
    large = np.minimum(large, T5_BUCKETS - 1)
    return np.where(dist < max_exact, dist, large).astype(np.int32)


N_BIAS_TILES = 4


def _bias_bucket_tiles():
    r = np.arange(DSA_TQ)[:, None]
    c = np.arange(DSA_TK)[None, :]
    tiles = []
    for t in range(N_BIAS_TILES):
        dist = t * DSA_TQ + r - c
        if t == N_BIAS_TILES - 1:
            dist = np.full_like(dist, T5_MAX_DIST)
        tiles.append(_t5_bucket_np(np.maximum(dist, 0)))
    return np.stack(tiles)


def _bias_kernel(tab_ref, bkt_ref, o_ref):
    h = pl.program_id(1)
    bkt = bkt_ref[0]
    acc = jnp.zeros(bkt.shape, F32)
    for b in range(T5_BUCKETS):
        acc = jnp.where(bkt == b, tab_ref[b, h], acc)
    o_ref[0, 0] = acc


def _bias_tiles(t5_table):
    bkt = jnp.asarray(_bias_bucket_tiles())
    return pl.pallas_call(
        _bias_kernel,
        grid=(N_BIAS_TILES, DSA_HEADS),
        in_specs=[pl.BlockSpec(memory_space=pltpu.SMEM),
                  pl.BlockSpec((1, DSA_TQ, DSA_TK), lambda t, h: (t, 0, 0))],
        out_specs=pl.BlockSpec((1, 1, DSA_TQ, DSA_TK), lambda t, h: (t, h, 0, 0)),
        out_shape=jax.ShapeDtypeStruct((N_BIAS_TILES, DSA_HEADS, DSA_TQ, DSA_TK), F32),
        compiler_params=_params("parallel", "parallel"),
        name="t5_bias",
    )(t5_table, bkt)


def _dsa_kernel(q_ref, qi_ref, sm_ref, k2_ref, c_ref, wuk_ref, wuv_ref, bias_ref, o_ref,
                keys_ref, qim_ref, qall_ref, p_ref, m_ref, l_ref, acc_ref, *, k_top):
    tq, tk = DSA_TQ, DSA_TK
    i = pl.program_id(1)
    nsb = (i * tq + tq + tk - 1) // tk
    lane = lax.broadcasted_iota(jnp.int32, (1, LANES), 1)
    qpos = i * tq + lax.broadcasted_iota(jnp.int32, (tq, 1), 0)
    kcol = lax.broadcasted_iota(jnp.int32, (1, tk), 1)
    zero = jnp.zeros((), BF16)

    w_s = sm_ref[0][:, :IDX_HEADS] * ((IDX_HEADS ** -0.5) * (IDX_DIM ** -0.5))
    for h in range(IDX_HEADS):
        qp = qi_ref[0, :, (h // 2) * LANES:(h // 2 + 1) * LANES]
        qim_ref[h] = jnp.where((lane // IDX_DIM) == (h % 2), qp, zero)

    def score_block(j, carry):
        start = pl.multiple_of(j * tk, tk)
        kb = k2_ref[0, pl.ds(start, tk), :]
        sc = jnp.zeros((tq, tk), F32)
        for h in range(IDX_HEADS):
            lg = lax.dot_general(qim_ref[h], kb, _CONTRACT_LAST, preferred_element_type=F32)
            sc = sc + w_s[:, h:h + 1] * jnp.maximum(lg, 0.0)
        sc = jnp.where(sc == 0.0, 0.0, sc)
        bits = pltpu.bitcast(sc, jnp.int32)
        key = bits ^ ((bits >> 31) & 0x7FFFFFFF)
        keys_ref[j] = jnp.where(start + kcol <= qpos, key, KEY_NEG_INF)
        return carry

    lax.fori_loop(0, nsb, score_block, 0)

    def count(pred):
        def body(j, acc):
            return acc + jnp.where(pred(keys_ref[j], j), 1.0, 0.0)
        per_lane = lax.fori_loop(0, nsb, body, jnp.zeros((tq, tk), F32))
        return jnp.sum(per_lane, axis=-1, keepdims=True)

    def bit_step(t, u):
        bit = lax.shift_left(jnp.int32(1), 31 - t)
        cand_u = u | bit
        cand = cand_u ^ INT_MIN
        n = count(lambda kk, j: kk >= cand)
        return jnp.where(n >= k_top, cand_u, u)

    u = lax.fori_loop(0, 32, bit_step, jnp.zeros((tq, 1), jnp.int32))
    thr = u ^ INT_MIN
    n_gt = count(lambda kk, j: kk > thr)
    n_ge = count(lambda kk, j: kk >= thr)
    need = k_top - n_gt
    excess = (n_ge > k_top) & (thr > KEY_NEG_INF)
    s_total = k2_ref.shape[1]

    def tie_search():
        def step(t, p):
            cand = p | lax.shift_left(jnp.int32(1), (s_total.bit_length() - 1) - t)
            n = count(lambda kk, j: (kk == thr) & (j * tk + kcol <= cand - 1))
            return jnp.where(n < need, cand, p)
        p = lax.fori_loop(0, s_total.bit_length(), step, jnp.zeros((tq, 1), jnp.int32))
        return jnp.where(excess, p, s_total)

    any_excess = jnp.max(jnp.where(excess, 1.0, 0.0)) > 0.0
    tie_limit = lax.cond(any_excess, tie_search, lambda: jnp.full((tq, 1), s_total, jnp.int32))

    for h in range(DSA_HEADS):
        qp = q_ref[0, :, (h // 2) * LANES:(h // 2 + 1) * LANES]
        ql = jnp.dot(qp, wuk_ref[h], preferred_element_type=F32) * (DSA_DH ** -0.5)
        qall_ref[h * tq:(h + 1) * tq, :] = ql.astype(BF16)
    m_ref[...] = jnp.full_like(m_ref, NEG_BIG)
    l_ref[...] = jnp.zeros_like(l_ref)
    acc_ref[...] = jnp.zeros_like(acc_ref)

    def attend(j, carry):
        start = pl.multiple_of(j * tk, tk)
        cb = c_ref[0, pl.ds(start, tk), :]
        s_all = lax.dot_general(qall_ref[...], cb, _CONTRACT_LAST, preferred_element_type=F32)
        key = keys_ref[j]
        kpos = start + kcol
        sel = ((key > thr) | ((key == thr) & (kpos <= tie_limit))) & (kpos <= qpos)
        tile = jnp.minimum((i * tq - start) // tq, N_BIAS_TILES - 1)
        for h in range(DSA_HEADS):
            rows = slice(h * tq, (h + 1) * tq)
            s = jnp.where(sel, s_all[rows] + bias_ref[tile, h], NEG_BIG)
            m_old = m_ref[rows]
            m_new = jnp.maximum(m_old, jnp.max(s, axis=-1, keepdims=True))
            alpha = jnp.exp(m_old - m_new)
            p = jnp.where(sel, jnp.exp(s - m_new), 0.0)
            l_ref[rows] = alpha * l_ref[rows] + jnp.sum(p, axis=-1, keepdims=True)
            m_ref[rows] = m_new
            p_ref[rows] = p.astype(BF16)
            acc_ref[rows] = alpha * acc_ref[rows]
        acc_ref[...] += jnp.dot(p_ref[...], cb, preferred_element_type=F32)
        return carry

    lax.fori_loop(0, nsb, attend, 0)

    for pair in range(DSA_HEADS // 2):
        out = jnp.zeros((tq, LANES), F32)
        for h in (2 * pair, 2 * pair + 1):
            rows = slice(h * tq, (h + 1) * tq)
            o_lat = (acc_ref[rows] / l_ref[rows]).astype(BF16)
            out = out + jnp.dot(o_lat, wuv_ref[h], preferred_element_type=F32)
        o_ref[0, :, pair * LANES:(pair + 1) * LANES] = out.astype(BF16)


def _dsa(big, small, c, w_uk, w_uv, bias_tiles, q_col, qi_col, k2_col):
    b, s, _ = big.shape
    k_top = min(TOPK_MAX, s // 4)
    hq = DSA_HEADS * DSA_DH
    hi = IDX_HEADS * IDX_DIM
    wuk = jnp.zeros((DSA_HEADS, LANES, DSA_LATENT), F32)
    wuv = jnp.zeros((DSA_HEADS, DSA_LATENT, LANES), F32)
    even = np.arange(0, DSA_HEADS, 2)
    wuk = wuk.at[even, :DSA_DH].set(w_uk[even]).at[even + 1, DSA_DH:].set(w_uk[even + 1]).astype(BF16)
    wuv = wuv.at[even, :, :DSA_DH].set(w_uv[even]).at[even + 1, :, DSA_DH:].set(w_uv[even + 1]).astype(BF16)
    nq = s // DSA_TQ
    rows = DSA_HEADS * DSA_TQ
    return pl.pallas_call(
        functools.partial(_dsa_kernel, k_top=k_top),
        grid=(b, nq),
        in_specs=[
            pl.BlockSpec((1, DSA_TQ, hq), lambda bi, qi: (bi, qi, q_col)),
            pl.BlockSpec((1, DSA_TQ, hi), lambda bi, qi: (bi, qi, qi_col)),
            pl.BlockSpec((1, DSA_TQ, LANES), lambda bi, qi: (bi, qi, 0)),
            pl.BlockSpec((1, s, LANES), lambda bi, qi: (bi, 0, k2_col)),
            pl.BlockSpec((1, s, DSA_LATENT), lambda bi, qi: (bi, 0, 0)),
            _resident(wuk.shape), _resident(wuv.shape), _resident(bias_tiles.shape),
        ],
        out_specs=pl.BlockSpec((1, DSA_TQ, hq), lambda bi, qi: (bi, qi, 0)),
        out_shape=jax.ShapeDtypeStruct((b, s, hq), BF16),
        scratch_shapes=[
            pltpu.VMEM((s // DSA_TK, DSA_TQ, DSA_TK), jnp.int32),
            pltpu.VMEM((IDX_HEADS, DSA_TQ, LANES), BF16),
            pltpu.VMEM((rows, DSA_LATENT), BF16),
            pltpu.VMEM((rows, DSA_TK), BF16),
            pltpu.VMEM((rows, 1), F32), pltpu.VMEM((rows, 1), F32),
            pltpu.VMEM((rows, DSA_LATENT), F32),
        ],
        compiler_params=_params("parallel", "arbitrary"),
        name="dsa",
    )(big, big, small, big, c, wuk, wuv, bias_tiles)


def _even_mixer(x2, b, s, g, w_in, w_gate, b_gate, norm_g, fox_b, w_out):
    hk = GLA_HEADS * GLA_DK
    hv = GLA_HEADS * GLA_DV
    hf = FOX_HEADS * FOX_DH
    o = np.cumsum([0, hk, hk, hv, hv, GLA_GATE_RANK, hf, hf, hf, FOX_HEADS])
    col = lambda n: w_in[:, o[n]:o[n + 1]]
    w_big = jnp.concatenate([col(0), col(1), col(2), col(3), col(5), col(6), col(7)], axis=1)
    w_small = jnp.zeros((w_in.shape[0], LANES), F32)
    w_small = w_small.at[:, :GLA_GATE_RANK].set(col(4)).at[:, GLA_GATE_RANK:GLA_GATE_RANK + FOX_HEADS].set(col(8))
    big, small = _inproj(x2, g, w_big, w_small)
    big = big.reshape(b, s, -1)
    small = small.reshape(b, s, LANES)
    o_gla = _gla(big, small, w_gate, b_gate, norm_g)
    f_logit_t = small[:, :, GLA_GATE_RANK:GLA_GATE_RANK + FOX_HEADS].transpose(0, 2, 1)
    f_cum = _fox_gate(f_logit_t, fox_b)
    fox_col = (2 * hk + 2 * hv) // LANES
    o_fox = _fox(big, f_cum, fox_col, fox_col + hf // LANES, fox_col + 2 * hf // LANES)
    return _outproj(x2, [o_gla.reshape(b * s, hv), o_fox.reshape(b * s, hf)], w_out)


def _odd_mixer(x2, b, s, g, w_in, kv_g, w_uk, w_uv, w_out, bias_tiles):
    hq = DSA_HEADS * DSA_DH
    hi = IDX_HEADS * IDX_DIM
    o = np.cumsum([0, hq, DSA_LATENT, hi, IDX_DIM, IDX_HEADS])
    col = lambda n: w_in[:, o[n]:o[n + 1]]
    w_big = jnp.concatenate([col(0), col(2), col(3), col(3)], axis=1)
    w_small = jnp.zeros((w_in.shape[0], LANES), F32).at[:, :IDX_HEADS].set(col(4))
    big, small, c = _inproj(x2, g, w_big, w_small, col(1), kv_g)
    big = big.reshape(b, s, -1)
    o_dsa = _dsa(big, small.reshape(b, s, LANES), c.reshape(b, s, DSA_LATENT), w_uk, w_uv, bias_tiles,
                 0, hq // hi, (hq + hi) // LANES)
    return _outproj(x2, [o_dsa.reshape(b * s, hq)], w_out)


def kernel(x, norm_g, ffn_w_in, ffn_w_out, even_w_in, gla_w_gate, gla_b_gate, gla_norm_g, fox_b_f, even_w_out,
           odd_w_in, mla_kv_norm_g, mla_w_uk, mla_w_uv, odd_w_out, t5_table, final_norm_g):
    b, s, d = x.shape
    depth = norm_g.shape[0]
    x2 = x.reshape(b * s, d)
    bias_tiles = _bias_tiles(t5_table) if depth > 1 else None
    for layer in range(depth):
        g = norm_g[layer]
        j = layer // 2
        x2 = _ffn(x2, g[0], ffn_w_in[layer, 0], ffn_w_out[layer, 0])
        if layer % 2 == 0:
            x2 = _even_mixer(x2, b, s, g[1], even_w_in[j], gla_w_gate[j], gla_b_gate[j], gla_norm_g[j],
                             fox_b_f[j], even_w_out[j])
        else:
            x2 = _odd_mixer(x2, b, s, g[1], odd_w_in[j], mla_kv_norm_g[j], mla_w_uk[j], mla_w_uv[j],
                            odd_w_out[j], bias_tiles)
        last = layer == depth - 1
        x2 = _ffn(x2, g[2], ffn_w_in[layer, 1], ffn_w_out[layer, 1], final_norm_g if last else None)
    return x2.reshape(b, s, d)
```

```python
import functools
import math

import numpy as np
import jax
import jax.numpy as jnp
from jax import lax
from jax.experimental import pallas as pl
from jax.experimental.pallas import tpu as pltpu

F32 = jnp.float32
BF16 = jnp.bfloat16

EPS = 1e-6
GLA_HEADS = 4
GLA_DK = 64
GLA_DV = 128
GLA_GATE_RANK = 16
GLA_GATE_NORMALIZER = 16.0
GLA_CHUNK = 64
FOX_HEADS = 8
FOX_DH = 64
DSA_HEADS = 16
DSA_DH = 64
DSA_LATENT = 256
IDX_HEADS = 8
IDX_DIM = 64
TOPK_MAX = 256
T5_BUCKETS = 32
T5_MAX_DIST = 128

LANES = 128
VMEM_LIMIT = 56 * 1024 * 1024

ROW_TILE = 512
FFN_CHUNK = 256
GLA_ROWS = 256
FOX_TQ = 256
DSA_TQ = 128
DSA_TK = 256
NEG_BIG = -1e30
INT_MIN = -2 ** 31
KEY_NEG_INF = INT_MIN + 0x7FFFFF

_CONTRACT_LAST = (((1,), (1,)), ((), ()))
_CONTRACT_FIRST = (((0,), (0,)), ((), ()))


def _params(*sem):
    return pltpu.CompilerParams(dimension_semantics=sem, vmem_limit_bytes=VMEM_LIMIT)


def _resident(shape):
    nd = len(shape)
    return pl.BlockSpec(shape, lambda *_: (0,) * nd, pipeline_mode=pl.Buffered(1))


def _rms(x, g):
    return x * lax.rsqrt(jnp.mean(x * x, axis=-1, keepdims=True) + EPS) * g


def _log_sigmoid(x):
    return jnp.minimum(x, 0.0) - jnp.log1p(jnp.exp(-jnp.abs(x)))


def _silu(x):
    return x * jax.nn.sigmoid(x)


def _ffn_kernel(*refs, n_chunks, final):
    if final:
        x_ref, g_ref, wa_ref, wb_ref, wo_ref, gf_ref, o_ref, h_ref, acc_ref = refs
    else:
        x_ref, g_ref, wa_ref, wb_ref, wo_ref, o_ref, h_ref, acc_ref = refs
    x = x_ref[...]
    h_ref[...] = _rms(x, g_ref[...]).astype(BF16)
    acc_ref[...] = jnp.zeros_like(acc_ref)

    def body(j, carry):
        h = h_ref[...]
        a = jnp.dot(h, wa_ref[j], preferred_element_type=F32)
        b = jnp.dot(h, wb_ref[j], preferred_element_type=F32)
        act = (_silu(a) * b).astype(BF16)
        acc_ref[...] += jnp.dot(act, wo_ref[j], preferred_element_type=F32)
        return carry

    lax.fori_loop(0, n_chunks, body, 0)
    y = x + 0.5 * acc_ref[...]
    if final:
        y = _rms(y, gf_ref[...])
    o_ref[...] = y


def _ffn(x2, g, w_in, w_out, final_g=None):
    t, d = x2.shape
    f = w_out.shape[0]
    n_chunks = f // FFN_CHUNK
    wa = w_in[:, :f].astype(BF16).reshape(d, n_chunks, FFN_CHUNK).transpose(1, 0, 2)
    wb = w_in[:, f:].astype(BF16).reshape(d, n_chunks, FFN_CHUNK).transpose(1, 0, 2)
    wo = w_out.astype(BF16).reshape(n_chunks, FFN_CHUNK, d)
    final = final_g is not None
    row = pl.BlockSpec((ROW_TILE, d), lambda i: (i, 0))
    in_specs = [row, _resident((1, d)), _resident(wa.shape), _resident(wb.shape), _resident(wo.shape)]
    args = [x2, g.reshape(1, d), wa, wb, wo]
    if final:
        in_specs.append(_resident((1, d)))
        args.append(final_g.reshape(1, d))
    return pl.pallas_call(
        functools.partial(_ffn_kernel, n_chunks=n_chunks, final=final),
        grid=(t // ROW_TILE,),
        in_specs=in_specs,
        out_specs=row,
        out_shape=jax.ShapeDtypeStruct((t, d), F32),
        scratch_shapes=[pltpu.VMEM((ROW_TILE, d), BF16), pltpu.VMEM((ROW_TILE, d), F32)],
        compiler_params=_params("parallel"),
        name="ffn",
    )(*args)


def _inproj_kernel(*refs, big_chunks, with_c):
    if with_c:
        x_ref, g_ref, wbig_ref, wsm_ref, wc_ref, gc_ref, big_ref, sm_ref, c_ref = refs
    else:
        x_ref, g_ref, wbig_ref, wsm_ref, big_ref, sm_ref = refs
    h = _rms(x_ref[...], g_ref[...]).astype(BF16)
    for lo, hi in big_chunks:
        big_ref[:, lo:hi] = jnp.dot(h, wbig_ref[:, lo:hi], preferred_element_type=F32).astype(BF16)
    sm_ref[...] = jnp.dot(h, wsm_ref[...], preferred_element_type=F32)
    if with_c:
        ckv = jnp.dot(h, wc_ref[...], preferred_element_type=F32)
        c_ref[...] = _rms(ckv, gc_ref[...]).astype(BF16)


def _inproj(x2, g, w_big, w_small, w_c=None, g_c=None):
    t, d = x2.shape
    nbig = w_big.shape[1]
    with_c = w_c is not None
    step = 4 * LANES
    big_chunks = tuple((lo, min(lo + step, nbig)) for lo in range(0, nbig, step))
    row = lambda n: pl.BlockSpec((ROW_TILE, n), lambda i: (i, 0))
    in_specs = [row(d), _resident((1, d)), _resident(w_big.shape), _resident(w_small.shape)]
    args = [x2, g.reshape(1, d), w_big.astype(BF16), w_small.astype(BF16)]
    out_specs = [row(nbig), row(LANES)]
    out_shape = [jax.ShapeDtypeStruct((t, nbig), BF16), jax.ShapeDtypeStruct((t, LANES), F32)]
    if with_c:
        nc = w_c.shape[1]
        in_specs += [_resident(w_c.shape), _resident((1, nc))]
        args += [w_c.astype(BF16), g_c.reshape(1, nc)]
        out_specs.append(row(nc))
        out_shape.append(jax.ShapeDtypeStruct((t, nc), BF16))
    return pl.pallas_call(
        functools.partial(_inproj_kernel, big_chunks=big_chunks, with_c=with_c),
        grid=(t // ROW_TILE,),
        in_specs=in_specs,
        out_specs=out_specs,
        out_shape=out_shape,
        compiler_params=_params("parallel"),
        name="inproj_c" if with_c else "inproj",
    )(*args)


def _outproj_kernel(*refs, n_in):
    x_ref = refs[0]
    o_refs = refs[1:1 + n_in]
    w_refs = refs[1 + n_in:1 + 2 * n_in]
    y_ref = refs[1 + 2 * n_in]
    y = x_ref[...]
    for o_ref, w_ref in zip(o_refs, w_refs):
        y = y + jnp.dot(o_ref[...], w_ref[...], preferred_element_type=F32)
    y_ref[...] = y


def _outproj(x2, outs, w_out):
    t, d = x2.shape
    ws, lo = [], 0
    for o in outs:
        ws.append(w_out[lo:lo + o.shape[1]].astype(BF16))
        lo += o.shape[1]
    row = lambda n: pl.BlockSpec((ROW_TILE, n), lambda i: (i, 0))
    return pl.pallas_call(
        functools.partial(_outproj_kernel, n_in=len(outs)),
        grid=(t // ROW_TILE,),
        in_specs=[row(d)] + [row(o.shape[1]) for o in outs] + [_resident(w.shape) for w in ws],
        out_specs=row(d),
        out_shape=jax.ShapeDtypeStruct((t, d), F32),
        compiler_params=_params("parallel"),
        name="outproj",
    )(x2, *outs, *ws)


def _gla_kernel(q_ref, k_ref, v_ref, go_ref, sm_ref, wg_ref, bg_ref, ng_ref, o_ref, st_ref):
    rs = q_ref.shape[1]
    c_len = GLA_CHUNK

    @pl.when(pl.program_id(1) == 0)
    def _():
        st_ref[...] = jnp.zeros_like(st_ref)

    pre = jnp.dot(sm_ref[0].astype(BF16), wg_ref[...], preferred_element_type=F32) + bg_ref[...]
    gk = _log_sigmoid(pre) / GLA_GATE_NORMALIZER
    r = lax.broadcasted_iota(jnp.int32, (rs, rs), 0)
    c = lax.broadcasted_iota(jnp.int32, (rs, rs), 1)
    same = (r // c_len) == (c // c_len)
    tri = jnp.where(same & (c <= r), 1.0, 0.0).astype(F32)
    ones = jnp.where(same, 1.0, 0.0).astype(F32)
    g_cum = jnp.dot(tri, gk, precision=lax.Precision.HIGHEST, preferred_element_type=F32)
    g_last = jnp.dot(ones, gk, precision=lax.Precision.HIGHEST, preferred_element_type=F32)
    q = q_ref[0].astype(F32) * (GLA_DK ** -0.5)
    k = k_ref[0].astype(F32)
    q_dec = (q * jnp.exp(g_cum)).astype(BF16)
    k_dec = (k * jnp.exp(-g_cum)).astype(BF16)
    k_end = (k * jnp.exp(g_last - g_cum)).astype(BF16)
    decay = jnp.exp(g_last)

    lane = lax.broadcasted_iota(jnp.int32, (1, LANES), 1)
    ri = lax.broadcasted_iota(jnp.int32, (c_len, c_len), 0)
    ci = lax.broadcasted_iota(jnp.int32, (c_len, c_len), 1)
    causal = ci <= ri
    ng = ng_ref[...]
    zero = jnp.zeros((), BF16)
    for ch in range(rs // c_len):
        rows = slice(ch * c_len, (ch + 1) * c_len)
        for h in range(GLA_HEADS):
            pair = slice((h // 2) * LANES, (h // 2 + 1) * LANES)
            mine = (lane // GLA_DK) == (h % 2)
            vcol = slice(h * GLA_DV, (h + 1) * GLA_DV)
            qm = jnp.where(mine, q_dec[rows, pair], zero)
            a = lax.dot_general(qm, k_dec[rows, pair], _CONTRACT_LAST, preferred_element_type=F32)
            a = jnp.where(causal, a, 0.0)
            vh = v_ref[0, rows, vcol]
            st = st_ref[h]
            o = jnp.dot(a.astype(BF16), vh, preferred_element_type=F32)
            o = o + lax.dot_general(qm, st.astype(BF16), _CONTRACT_LAST, preferred_element_type=F32)
            km = jnp.where(mine, k_end[rows, pair], zero)
            upd = lax.dot_general(vh, km, _CONTRACT_FIRST, preferred_element_type=F32)
            st_ref[h] = st * decay[ch * c_len:ch * c_len + 1, pair] + upd
            on = _rms(o, ng)
            gate = go_ref[0, rows, vcol].astype(F32)
            o_ref[0, rows, vcol] = (on * _silu(gate)).astype(BF16)


def _gla(big, small, w_gate, b_gate, norm_g):
    b, s, _ = big.shape
    hk = GLA_HEADS * GLA_DK
    hv = GLA_HEADS * GLA_DV
    wg = jnp.zeros((LANES, hk), F32).at[:GLA_GATE_RANK].set(w_gate).astype(BF16)
    blk = lambda n, cb: pl.BlockSpec((1, GLA_ROWS, n), lambda bi, si: (bi, si, cb))
    return pl.pallas_call(
        _gla_kernel,
        grid=(b, s // GLA_ROWS),
        in_specs=[blk(hk, 0), blk(hk, 1), blk(hv, 1), blk(hv, 2), blk(LANES, 0),
                  _resident(wg.shape), _resident((1, hk)), _resident((1, GLA_DV))],
        out_specs=blk(hv, 0),
        out_shape=jax.ShapeDtypeStruct((b, s, hv), BF16),
        scratch_shapes=[pltpu.VMEM((GLA_HEADS, GLA_DV, LANES), F32)],
        compiler_params=_params("parallel", "arbitrary"),
        name="gla",
    )(big, big, big, big, small, wg, b_gate.reshape(1, hk), norm_g.reshape(1, GLA_DV))


def _foxgate_kernel(fl_ref, b_ref, f_ref):
    x = _log_sigmoid(fl_ref[0] + b_ref[...])
    s = x.shape[1]
    lane = lax.broadcasted_iota(jnp.int32, x.shape, 1)
    sh = 1
    while sh < s:
        x = x + jnp.where(lane >= sh, pltpu.roll(x, sh, 1), 0.0)
        sh *= 2
    f_ref[0] = x


def _fox_gate(f_logit_t, bias):
    b, h, s = f_logit_t.shape
    blk = pl.BlockSpec((1, h, s), lambda bi: (bi, 0, 0))
    return pl.pallas_call(
        _foxgate_kernel,
        grid=(b,),
        in_specs=[blk, _resident((h, 1))],
        out_specs=blk,
        out_shape=jax.ShapeDtypeStruct((b, h, s), F32),
        compiler_params=_params("parallel"),
        name="fox_gate",
    )(f_logit_t, bias.reshape(h, 1))


def _fox_kernel(q_ref, k_ref, v_ref, f_ref, o_ref, qs_ref, mx_ref, l_ref, acc_ref):
    tq = q_ref.shape[1]
    tk = tq
    pr = pl.program_id(1)
    qi = pl.program_id(2)
    lane = lax.broadcasted_iota(jnp.int32, (1, LANES), 1)
    q = q_ref[0] * jnp.asarray(FOX_DH ** -0.5, BF16)
    zero = jnp.zeros((), BF16)
    for hh in range(2):
        qs_ref[hh * tq:(hh + 1) * tq, :] = jnp.where((lane // FOX_DH) == hh, q, zero)
    mx_ref[...] = jnp.full_like(mx_ref, NEG_BIG)
    l_ref[...] = jnp.zeros_like(l_ref)
    acc_ref[...] = jnp.zeros_like(acc_ref)
    row = lax.broadcasted_iota(jnp.int32, (tq, tk), 0)
    col = lax.broadcasted_iota(jnp.int32, (tq, tk), 1)

    def logits(j, hh, s_all, diagonal):
        fk = f_ref[0, 2 * pr + hh, pl.ds(j, 1), :]
        s = s_all[hh * tq:(hh + 1) * tq] - fk
        if diagonal:
            s = jnp.where(col <= row, s, NEG_BIG)
        return s

    def max_block(j, diagonal):
        kb = k_ref[0, pl.ds(pl.multiple_of(j * tk, tk), tk), :]
        s_all = lax.dot_general(qs_ref[...], kb, _CONTRACT_LAST, preferred_element_type=F32)
        for hh in range(2):
            rows = slice(hh * tq, (hh + 1) * tq)
            mx_ref[rows] = jnp.maximum(mx_ref[rows], logits(j, hh, s_all, diagonal))

    def sum_block(j, diagonal):
        start = pl.multiple_of(j * tk, tk)
        kb = k_ref[0, pl.ds(start, tk), :]
        vb = v_ref[0, pl.ds(start, tk), :]
        s_all = lax.dot_general(qs_ref[...], kb, _CONTRACT_LAST, preferred_element_type=F32)
        for hh in range(2):
            rows = slice(hh * tq, (hh + 1) * tq)
            p = jnp.exp(logits(j, hh, s_all, diagonal) - mx_ref[rows])
            l_ref[rows] += p
            acc_ref[rows] += jnp.dot(p.astype(BF16), vb, preferred_element_type=F32)

    def loop(fn):
        def body(j, carry):
            fn(j, False)
            return carry
        lax.fori_loop(0, qi, body, 0)
        fn(qi, True)

    loop(max_block)
    for hh in range(2):
        rows = slice(hh * tq, (hh + 1) * tq)
        mx_ref[rows] = jnp.broadcast_to(jnp.max(mx_ref[rows], axis=-1, keepdims=True), (tq, tk))
    loop(sum_block)
    o = [acc_ref[hh * tq:(hh + 1) * tq] / jnp.sum(l_ref[hh * tq:(hh + 1) * tq], axis=-1, keepdims=True)
         for hh in range(2)]
    o_ref[0] = jnp.where((lane // FOX_DH) == 0, o[0], o[1]).astype(BF16)


def _fox(big, f_cum, q_col, k_col, v_col):
    b, s, _ = big.shape
    n_pairs = FOX_HEADS // 2
    nkb = s // FOX_TQ
    f4 = f_cum.reshape(b, FOX_HEADS, nkb, FOX_TQ)
    return pl.pallas_call(
        _fox_kernel,
        grid=(b, n_pairs, s // FOX_TQ),
        in_specs=[
            pl.BlockSpec((1, FOX_TQ, LANES), lambda bi, pi, qi: (bi, qi, q_col + pi)),
            pl.BlockSpec((1, s, LANES), lambda bi, pi, qi: (bi, 0, k_col + pi)),
            pl.BlockSpec((1, s, LANES), lambda bi, pi, qi: (bi, 0, v_col + pi)),
            pl.BlockSpec((1, FOX_HEADS, nkb, FOX_TQ), lambda bi, pi, qi: (bi, 0, 0, 0)),
        ],
        out_specs=pl.BlockSpec((1, FOX_TQ, LANES), lambda bi, pi, qi: (bi, qi, pi)),
        out_shape=jax.ShapeDtypeStruct((b, s, FOX_HEADS * FOX_DH), BF16),
        scratch_shapes=[pltpu.VMEM((2 * FOX_TQ, LANES), BF16),
                        pltpu.VMEM((2 * FOX_TQ, FOX_TQ), F32),
                        pltpu.VMEM((2 * FOX_TQ, FOX_TQ), F32),
                        pltpu.VMEM((2 * FOX_TQ, LANES), F32)],
        compiler_params=_params("parallel", "parallel", "arbitrary"),
        name="fox",
    )(big, big, big, f4)


def _t5_bucket_np(dist):
    max_exact = T5_BUCKETS // 2
    d = np.maximum(dist, 1).astype(np.float32)
    large = max_exact + (np.log(d / np.float32(max_exact)) / np.float32(math.log(T5_MAX_DIST / max_exact))
                         * np.float32(T5_BUCKETS - max_exact)).astype(np.int32)
    large = np.minimum(large, T5_BUCKETS - 1)
    return np.where(dist < max_exact, dist, large).astype(np.int32)


N_BIAS_TILES = -(-(DSA_TK + T5_MAX_DIST - 1) // DSA_TQ)
LOG2E = math.log2(math.e)


def _bias_bucket_tiles():
    r = np.arange(DSA_TQ)[:, None]
    c = np.arange(DSA_TK)[None, :]
    return np.stack([_t5_bucket_np(np.maximum(t * DSA_TQ + r - c, 0)) for t in range(N_BIAS_TILES)])


def _bias_kernel(tab_ref, bkt_ref, o_ref):
    h = pl.program_id(1)
    bkt = bkt_ref[0]
    acc = jnp.zeros(bkt.shape, F32)
    for b in range(T5_BUCKETS):
        acc = jnp.where(bkt == b, tab_ref[b, h], acc)
    o_ref[0, 0] = (acc - tab_ref[T5_BUCKETS - 1, h]) * LOG2E


def _bias_tiles(t5_table):
    bkt = jnp.asarray(_bias_bucket_tiles())
    return pl.pallas_call(
        _bias_kernel,
        grid=(N_BIAS_TILES, DSA_HEADS),
        in_specs=[pl.BlockSpec(memory_space=pltpu.SMEM),
                  pl.BlockSpec((1, DSA_TQ, DSA_TK), lambda t, h: (t, 0, 0))],
        out_specs=pl.BlockSpec((1, 1, DSA_TQ, DSA_TK), lambda t, h: (t, h, 0, 0)),
        out_shape=jax.ShapeDtypeStruct((N_BIAS_TILES, DSA_HEADS, DSA_TQ, DSA_TK), F32),
        compiler_params=_params("parallel", "parallel"),
        name="t5_bias",
    )(t5_table, bkt)


def _dsa_kernel(q_ref, qi_ref, wt_ref, k2_ref, c_ref, wuk_ref, wuv_ref, bias_ref, o_ref,
                keyt_ref, selb_ref, qim_ref, qall_ref, p_ref, mx_ref, l_ref, acc_ref, *, k_top):
    tq, tk = DSA_TQ, DSA_TK
    sub = 8
    i = pl.program_id(1)
    nsb = (i * tq + tq + tk - 1) // tk
    lane = lax.broadcasted_iota(jnp.int32, (1, LANES), 1)
    qpos = i * tq + lax.broadcasted_iota(jnp.int32, (1, tq), 1)
    krow = lax.broadcasted_iota(jnp.int32, (tk, 1), 0)
    zero = jnp.zeros((), BF16)

    w_s = wt_ref[0] * ((IDX_HEADS ** -0.5) * (IDX_DIM ** -0.5))
    for h in range(IDX_HEADS):
        qp = qi_ref[0, :, (h // 2) * LANES:(h // 2 + 1) * LANES]
        qim_ref[h] = jnp.where((lane // IDX_DIM) == (h % 2), qp, zero)

    def score_block(j, carry):
        start = pl.multiple_of(j * tk, tk)
        kb = k2_ref[0, pl.ds(start, tk), :]
        sc = jnp.zeros((tk, tq), F32)
        for h in range(IDX_HEADS):
            lg = lax.dot_general(kb, qim_ref[h], _CONTRACT_LAST, preferred_element_type=F32)
            sc = sc + w_s[h:h + 1, :] * jnp.maximum(lg, 0.0)
        sc = jnp.where(sc == 0.0, 0.0, sc)
        bits = pltpu.bitcast(sc, jnp.int32)
        key = bits ^ ((bits >> 31) & 0x7FFFFFFF)
        keyt_ref[j] = jnp.where(start + krow <= qpos, key, KEY_NEG_INF)
        return carry

    lax.fori_loop(0, nsb, score_block, 0)

    def count(pred):
        def body(j, acc):
            x = jnp.where(pred(keyt_ref[j], j), 1.0, 0.0)
            parts = [x[r * sub:(r + 1) * sub] for r in range(tk // sub)]
            while len(parts) > 1:
                parts = [a + b for a, b in zip(parts[::2], parts[1::2])]
            return acc + parts[0]
        per_sublane = lax.fori_loop(0, nsb, body, jnp.zeros((sub, tq), F32))
        return jnp.sum(per_sublane, axis=0, keepdims=True)

    def bit_step(t, u):
        bit = lax.shift_left(jnp.int32(1), 31 - t)
        cand_u = u | bit
        cand = cand_u ^ INT_MIN
        n = count(lambda kk, j: kk >= cand)
        return jnp.where(n >= k_top, cand_u, u)

    u = lax.fori_loop(0, 32, bit_step, jnp.zeros((1, tq), jnp.int32))
    thr = u ^ INT_MIN
    n_gt = count(lambda kk, j: kk > thr)
    n_ge = count(lambda kk, j: kk >= thr)
    need = k_top - n_gt
    excess = (n_ge > k_top) & (thr > KEY_NEG_INF)
    s_total = k2_ref.shape[1]

    def tie_search():
        def step(t, p):
            cand = p | lax.shift_left(jnp.int32(1), (s_total.bit_length() - 1) - t)
            n = count(lambda kk, j: (kk == thr) & (j * tk + krow <= cand - 1))
            return jnp.where(n < need, cand, p)
        p = lax.fori_loop(0, s_total.bit_length(), step, jnp.zeros((1, tq), jnp.int32))
        return jnp.where(excess, p, s_total)

    any_excess = jnp.max(jnp.where(excess, 1.0, 0.0)) > 0.0
    tie_limit = lax.cond(any_excess, tie_search, lambda: jnp.full((1, tq), s_total, jnp.int32))

    def select_block(j, carry):
        key = keyt_ref[j]
        kpos = j * tk + krow
        sel = ((key > thr) | ((key == thr) & (kpos <= tie_limit))) & (kpos <= qpos)
        selb_ref[j] = jnp.where(sel, 0.0, NEG_BIG).T
        return carry

    lax.fori_loop(0, nsb, select_block, 0)

    for h in range(DSA_HEADS):
        qp = q_ref[0, :, (h // 2) * LANES:(h // 2 + 1) * LANES]
        ql = jnp.dot(qp, wuk_ref[h], preferred_element_type=F32) * ((DSA_DH ** -0.5) * LOG2E)
        qall_ref[h * tq:(h + 1) * tq, :] = ql.astype(BF16)
    mx_ref[...] = jnp.full_like(mx_ref, NEG_BIG)
    l_ref[...] = jnp.zeros_like(l_ref)
    acc_ref[...] = jnp.zeros_like(acc_ref)
    n_far = jnp.maximum(((i - N_BIAS_TILES) * tq) // tk + 1, 0)

    def logits(j, near):
        start = pl.multiple_of(j * tk, tk)
        cb = c_ref[0, pl.ds(start, tk), :]
        s_all = lax.dot_general(qall_ref[...], cb, _CONTRACT_LAST, preferred_element_type=F32)
        selb = selb_ref[j]
        tile = (i * tq - start) // tq
        for h in range(DSA_HEADS):
            rows = slice(h * tq, (h + 1) * tq)
            s = s_all[rows] + selb
            if near:
                s = s + bias_ref[tile, h]
            yield rows, s, cb

    def max_block(j, near):
        for rows, s, _ in logits(j, near):
            mx_ref[rows] = jnp.maximum(mx_ref[rows], s)

    def sum_block(j, near):
        for rows, s, cb in logits(j, near):
            p = jnp.exp2(s - mx_ref[rows])
            l_ref[rows] += p
            p_ref[rows] = p.astype(BF16)
        acc_ref[...] += jnp.dot(p_ref[...], cb, preferred_element_type=F32)

    def sweep(fn):
        def far(j, carry):
            fn(j, False)
            return carry

        def near(j, carry):
            fn(j, True)
            return carry
        lax.fori_loop(0, n_far, far, 0)
        lax.fori_loop(n_far, nsb, near, 0)

    sweep(max_block)
    for h in range(DSA_HEADS):
        rows = slice(h * tq, (h + 1) * tq)
        mx_ref[rows] = jnp.broadcast_to(jnp.max(mx_ref[rows], axis=-1, keepdims=True), (tq, tk))
    sweep(sum_block)

    for pair in range(DSA_HEADS // 2):
        out = jnp.zeros((tq, LANES), F32)
        for h in (2 * pair, 2 * pair + 1):
            rows = slice(h * tq, (h + 1) * tq)
            o_lat = (acc_ref[rows] / jnp.sum(l_ref[rows], axis=-1, keepdims=True)).astype(BF16)
            out = out + jnp.dot(o_lat, wuv_ref[h], preferred_element_type=F32)
        o_ref[0, :, pair * LANES:(pair + 1) * LANES] = out.astype(BF16)


def _dsa(big, w_idx_t, c, w_uk, w_uv, bias_tiles, q_col, qi_col, k2_col):
    b, s, _ = big.shape
    k_top = min(TOPK_MAX, s // 4)
    hq = DSA_HEADS * DSA_DH
    hi = IDX_HEADS * IDX_DIM
    wuk = jnp.zeros((DSA_HEADS, LANES, DSA_LATENT), F32)
    wuv = jnp.zeros((DSA_HEADS, DSA_LATENT, LANES), F32)
    even = np.arange(0, DSA_HEADS, 2)
    wuk = wuk.at[even, :DSA_DH].set(w_uk[even]).at[even + 1, DSA_DH:].set(w_uk[even + 1]).astype(BF16)
    wuv = wuv.at[even, :, :DSA_DH].set(w_uv[even]).at[even + 1, :, DSA_DH:].set(w_uv[even + 1]).astype(BF16)
    nq = s // DSA_TQ
    rows = DSA_HEADS * DSA_TQ
    return pl.pallas_call(
        functools.partial(_dsa_kernel, k_top=k_top),
        grid=(b, nq),
        in_specs=[
            pl.BlockSpec((1, DSA_TQ, hq), lambda bi, qi: (bi, qi, q_col)),
            pl.BlockSpec((1, DSA_TQ, hi), lambda bi, qi: (bi, qi, qi_col)),
            pl.BlockSpec((1, IDX_HEADS, DSA_TQ), lambda bi, qi: (bi, 0, qi)),
            pl.BlockSpec((1, s, LANES), lambda bi, qi: (bi, 0, k2_col)),
            pl.BlockSpec((1, s, DSA_LATENT), lambda bi, qi: (bi, 0, 0)),
            _resident(wuk.shape), _resident(wuv.shape), _resident(bias_tiles.shape),
        ],
        out_specs=pl.BlockSpec((1, DSA_TQ, hq), lambda bi, qi: (bi, qi, 0)),
        out_shape=jax.ShapeDtypeStruct((b, s, hq), BF16),
        scratch_shapes=[
            pltpu.VMEM((s // DSA_TK, DSA_TK, DSA_TQ), jnp.int32),
            pltpu.VMEM((s // DSA_TK, DSA_TQ, DSA_TK), F32),
            pltpu.VMEM((IDX_HEADS, DSA_TQ, LANES), BF16),
            pltpu.VMEM((rows, DSA_LATENT), BF16),
            pltpu.VMEM((rows, DSA_TK), BF16),
            pltpu.VMEM((rows, DSA_TK), F32),
            pltpu.VMEM((rows, DSA_TK), F32),
            pltpu.VMEM((rows, DSA_LATENT), F32),
        ],
        compiler_params=_params("parallel", "arbitrary"),
        name="dsa",
    )(big, big, w_idx_t, big, c, wuk, wuv, bias_tiles)


def _even_mixer(x2, b, s, g, w_in, w_gate, b_gate, norm_g, fox_b, w_out):
    hk = GLA_HEADS * GLA_DK
    hv = GLA_HEADS * GLA_DV
    hf = FOX_HEADS * FOX_DH
    o = np.cumsum([0, hk, hk, hv, hv, GLA_GATE_RANK, hf, hf, hf, FOX_HEADS])
    col = lambda n: w_in[:, o[n]:o[n + 1]]
    w_big = jnp.concatenate([col(0), col(1), col(2), col(3), col(5), col(6), col(7)], axis=1)
    w_small = jnp.zeros((w_in.shape[0], LANES), F32)
    w_small = w_small.at[:, :GLA_GATE_RANK].set(col(4)).at[:, GLA_GATE_RANK:GLA_GATE_RANK + FOX_HEADS].set(col(8))
    big, small = _inproj(x2, g, w_big, w_small)
    big = big.reshape(b, s, -1)
    small = small.reshape(b, s, LANES)
    o_gla = _gla(big, small, w_gate, b_gate, norm_g)
    f_logit_t = small[:, :, GLA_GATE_RANK:GLA_GATE_RANK + FOX_HEADS].transpose(0, 2, 1)
    f_cum = _fox_gate(f_logit_t, fox_b)
    fox_col = (2 * hk + 2 * hv) // LANES
    o_fox = _fox(big, f_cum, fox_col, fox_col + hf // LANES, fox_col + 2 * hf // LANES)
    return _outproj(x2, [o_gla.reshape(b * s, hv), o_fox.reshape(b * s, hf)], w_out)


def _odd_mixer(x2, b, s, g, w_in, kv_g, w_uk, w_uv, w_out, bias_tiles):
    hq = DSA_HEADS * DSA_DH
    hi = IDX_HEADS * IDX_DIM
    o = np.cumsum([0, hq, DSA_LATENT, hi, IDX_DIM, IDX_HEADS])
    col = lambda n: w_in[:, o[n]:o[n + 1]]
    w_big = jnp.concatenate([col(0), col(2), col(3), col(3)], axis=1)
    w_small = jnp.zeros((w_in.shape[0], LANES), F32).at[:, :IDX_HEADS].set(col(4))
    big, small, c = _inproj(x2, g, w_big, w_small, col(1), kv_g)
    big = big.reshape(b, s, -1)
    w_idx_t = small.reshape(b, s, LANES)[:, :, :IDX_HEADS].transpose(0, 2, 1)
    o_dsa = _dsa(big, w_idx_t, c.reshape(b, s, DSA_LATENT), w_uk, w_uv, bias_tiles,
                 0, hq // hi, (hq + hi) // LANES)
    return _outproj(x2, [o_dsa.reshape(b * s, hq)], w_out)


def kernel(x, norm_g, ffn_w_in, ffn_w_out, even_w_in, gla_w_gate, gla_b_gate, gla_norm_g, fox_b_f, even_w_out,
           odd_w_in, mla_kv_norm_g, mla_w_uk, mla_w_uv, odd_w_out, t5_table, final_norm_g):
    b, s, d = x.shape
    depth = norm_g.shape[0]
    x2 = x.reshape(b * s, d)
    bias_tiles = _bias_tiles(t5_table) if depth > 1 else None
    for layer in range(depth):
        g = norm_g[layer]
        j = layer // 2
        x2 = _ffn(x2, g[0], ffn_w_in[layer, 0], ffn_w_out[layer, 0])
        if layer % 2 == 0:
            x2 = _even_mixer(x2, b, s, g[1], even_w_in[j], gla_w_gate[j], gla_b_gate[j], gla_norm_g[j],
                             fox_b_f[j], even_w_out[j])
        else:
            x2 = _odd_mixer(x2, b, s, g[1], odd_w_in[j], mla_kv_norm_g[j], mla_w_uk[j], mla_w_uv[j],
                            odd_w_out[j], bias_tiles)
        last = layer == depth - 1
        x2 = _ffn(x2, g[2], ffn_w_in[layer, 1], ffn_w_out[layer, 1], final_norm_g if last else None)
    return x2.reshape(b, s, d)
```

```python
import functools
import math

import numpy as np
import jax
import jax.numpy as jnp
from jax import lax
from jax.experimental import pallas as pl
from jax.experimental.pallas import tpu as pltpu

F32 = jnp.float32
BF16 = jnp.bfloat16

EPS = 1e-6
GLA_HEADS = 4
GLA_DK = 64
GLA_DV = 128
GLA_GATE_RANK = 16
GLA_GATE_NORMALIZER = 16.0
GLA_CHUNK = 64
FOX_HEADS = 8
FOX_DH = 64
DSA_HEADS = 16
DSA_DH = 64
DSA_LATENT = 256
IDX_HEADS = 8
IDX_DIM = 64
TOPK_MAX = 256
T5_BUCKETS = 32
T5_MAX_DIST = 128

LANES = 128
VMEM_LIMIT = 56 * 1024 * 1024

ROW_TILE = 512
FFN_ROW_TILE = 1024
FFN_CHUNK = 256
GLA_ROWS = 256
FOX_TQ = 512
DSA_TQ = 128
DSA_TK = 256
NEG_BIG = -1e30
INT_MIN = -2 ** 31
KEY_NEG_INF = INT_MIN + 0x7FFFFF
HALF = 1 << 15

_CONTRACT_LAST = (((1,), (1,)), ((), ()))
_CONTRACT_FIRST = (((0,), (0,)), ((), ()))


def _params(*sem):
    return pltpu.CompilerParams(dimension_semantics=sem, vmem_limit_bytes=VMEM_LIMIT)


def _resident(shape):
    nd = len(shape)
    return pl.BlockSpec(shape, lambda *_: (0,) * nd, pipeline_mode=pl.Buffered(1))


def _rms(x, g):
    return x * lax.rsqrt(jnp.mean(x * x, axis=-1, keepdims=True) + EPS) * g


def _log_sigmoid(x):
    return jnp.minimum(x, 0.0) - jnp.log1p(jnp.exp(-jnp.abs(x)))


def _silu(x):
    return x * jax.nn.sigmoid(x)


def _ffn_kernel(*refs, n_chunks, final):
    if final:
        x_ref, g_ref, wa_ref, wb_ref, wo_ref, gf_ref, o_ref, h_ref, acc_ref = refs
    else:
        x_ref, g_ref, wa_ref, wb_ref, wo_ref, o_ref, h_ref, acc_ref = refs
    x = x_ref[...]
    h_ref[...] = _rms(x, g_ref[...]).astype(BF16)
    acc_ref[...] = jnp.zeros_like(acc_ref)

    def body(j, carry):
        h = h_ref[...]
        a = jnp.dot(h, wa_ref[j], preferred_element_type=F32)
        b = jnp.dot(h, wb_ref[j], preferred_element_type=F32)
        act = (_silu(a) * b).astype(BF16)
        acc_ref[...] += jnp.dot(act, wo_ref[j], preferred_element_type=F32)
        return carry

    lax.fori_loop(0, n_chunks, body, 0)
    y = x + 0.5 * acc_ref[...]
    if final:
        y = _rms(y, gf_ref[...])
    o_ref[...] = y


def _ffn(x2, g, w_in, w_out, final_g=None):
    t, d = x2.shape
    f = w_out.shape[0]
    n_chunks = f // FFN_CHUNK
    wa = w_in[:, :f].astype(BF16).reshape(d, n_chunks, FFN_CHUNK).transpose(1, 0, 2)
    wb = w_in[:, f:].astype(BF16).reshape(d, n_chunks, FFN_CHUNK).transpose(1, 0, 2)
    wo = w_out.astype(BF16).reshape(n_chunks, FFN_CHUNK, d)
    final = final_g is not None
    row = pl.BlockSpec((FFN_ROW_TILE, d), lambda i: (i, 0))
    in_specs = [row, _resident((1, d)), _resident(wa.shape), _resident(wb.shape), _resident(wo.shape)]
    args = [x2, g.reshape(1, d), wa, wb, wo]
    if final:
        in_specs.append(_resident((1, d)))
        args.append(final_g.reshape(1, d))
    return pl.pallas_call(
        functools.partial(_ffn_kernel, n_chunks=n_chunks, final=final),
        grid=(t // FFN_ROW_TILE,),
        in_specs=in_specs,
        out_specs=row,
        out_shape=jax.ShapeDtypeStruct((t, d), F32),
        scratch_shapes=[pltpu.VMEM((FFN_ROW_TILE, d), BF16), pltpu.VMEM((FFN_ROW_TILE, d), F32)],
        compiler_params=_params("parallel"),
        name="ffn",
    )(*args)


def _inproj_kernel(*refs, big_chunks, with_c):
    if with_c:
        x_ref, g_ref, wbig_ref, wsm_ref, wc_ref, gc_ref, big_ref, sm_ref, c_ref = refs
    else:
        x_ref, g_ref, wbig_ref, wsm_ref, big_ref, sm_ref = refs
    h = _rms(x_ref[...], g_ref[...]).astype(BF16)
    for lo, hi in big_chunks:
        big_ref[:, lo:hi] = jnp.dot(h, wbig_ref[:, lo:hi], preferred_element_type=F32).astype(BF16)
    sm_ref[...] = jnp.dot(h, wsm_ref[...], preferred_element_type=F32)
    if with_c:
        ckv = jnp.dot(h, wc_ref[...], preferred_element_type=F32)
        c_ref[...] = _rms(ckv, gc_ref[...]).astype(BF16)


def _inproj(x2, g, w_big, w_small, w_c=None, g_c=None):
    t, d = x2.shape
    nbig = w_big.shape[1]
    with_c = w_c is not None
    step = 4 * LANES
    big_chunks = tuple((lo, min(lo + step, nbig)) for lo in range(0, nbig, step))
    row = lambda n: pl.BlockSpec((ROW_TILE, n), lambda i: (i, 0))
    in_specs = [row(d), _resident((1, d)), _resident(w_big.shape), _resident(w_small.shape)]
    args = [x2, g.reshape(1, d), w_big.astype(BF16), w_small.astype(BF16)]
    out_specs = [row(nbig), row(LANES)]
    out_shape = [jax.ShapeDtypeStruct((t, nbig), BF16), jax.ShapeDtypeStruct((t, LANES), F32)]
    if with_c:
        nc = w_c.shape[1]
        in_specs += [_resident(w_c.shape), _resident((1, nc))]
        args += [w_c.astype(BF16), g_c.reshape(1, nc)]
        out_specs.append(row(nc))
        out_shape.append(jax.ShapeDtypeStruct((t, nc), BF16))
    return pl.pallas_call(
        functools.partial(_inproj_kernel, big_chunks=big_chunks, with_c=with_c),
        grid=(t // ROW_TILE,),
        in_specs=in_specs,
        out_specs=out_specs,
        out_shape=out_shape,
        compiler_params=_params("parallel"),
        name="inproj_c" if with_c else "inproj",
    )(*args)


def _outproj_kernel(*refs, n_in):
    x_ref = refs[0]
    o_refs = refs[1:1 + n_in]
    w_refs = refs[1 + n_in:1 + 2 * n_in]
    y_ref = refs[1 + 2 * n_in]
    y = x_ref[...]
    for o_ref, w_ref in zip(o_refs, w_refs):
        y = y + jnp.dot(o_ref[...], w_ref[...], preferred_element_type=F32)
    y_ref[...] = y


def _outproj(x2, outs, w_out):
    t, d = x2.shape
    ws, lo = [], 0
    for o in outs:
        ws.append(w_out[lo:lo + o.shape[1]].astype(BF16))
        lo += o.shape[1]
    row = lambda n: pl.BlockSpec((ROW_TILE, n), lambda i: (i, 0))
    return pl.pallas_call(
        functools.partial(_outproj_kernel, n_in=len(outs)),
        grid=(t // ROW_TILE,),
        in_specs=[row(d)] + [row(o.shape[1]) for o in outs] + [_resident(w.shape) for w in ws],
        out_specs=row(d),
        out_shape=jax.ShapeDtypeStruct((t, d), F32),
        compiler_params=_params("parallel"),
        name="outproj",
    )(x2, *outs, *ws)


def _gla_kernel(q_ref, k_ref, v_ref, go_ref, sm_ref, wg_ref, bg_ref, ng_ref, o_ref, st_ref):
    rs = q_ref.shape[1]
    c_len = GLA_CHUNK

    @pl.when(pl.program_id(1) == 0)
    def _():
        st_ref[...] = jnp.zeros_like(st_ref)

    pre = jnp.dot(sm_ref[0].astype(BF16), wg_ref[...], preferred_element_type=F32) + bg_ref[...]
    gk = _log_sigmoid(pre) / GLA_GATE_NORMALIZER
    r = lax.broadcasted_iota(jnp.int32, (rs, rs), 0)
    c = lax.broadcasted_iota(jnp.int32, (rs, rs), 1)
    same = (r // c_len) == (c // c_len)
    tri = jnp.where(same & (c <= r), 1.0, 0.0).astype(F32)
    ones = jnp.where(same, 1.0, 0.0).astype(F32)
    g_cum = jnp.dot(tri, gk, precision=lax.Precision.HIGHEST, preferred_element_type=F32)
    g_last = jnp.dot(ones, gk, precision=lax.Precision.HIGHEST, preferred_element_type=F32)
    q = q_ref[0].astype(F32) * (GLA_DK ** -0.5)
    k = k_ref[0].astype(F32)
    q_dec = (q * jnp.exp(g_cum)).astype(BF16)
    k_dec = (k * jnp.exp(-g_cum)).astype(BF16)
    k_end = (k * jnp.exp(g_last - g_cum)).astype(BF16)
    decay = jnp.exp(g_last)

    lane = lax.broadcasted_iota(jnp.int32, (1, LANES), 1)
    ri = lax.broadcasted_iota(jnp.int32, (c_len, c_len), 0)
    ci = lax.broadcasted_iota(jnp.int32, (c_len, c_len), 1)
    causal = ci <= ri
    ng = ng_ref[...]
    zero = jnp.zeros((), BF16)
    for ch in range(rs // c_len):
        rows = slice(ch * c_len, (ch + 1) * c_len)
        for h in range(GLA_HEADS):
            pair = slice((h // 2) * LANES, (h // 2 + 1) * LANES)
            mine = (lane // GLA_DK) == (h % 2)
            vcol = slice(h * GLA_DV, (h + 1) * GLA_DV)
            qm = jnp.where(mine, q_dec[rows, pair], zero)
            a = lax.dot_general(qm, k_dec[rows, pair], _CONTRACT_LAST, preferred_element_type=F32)
            a = jnp.where(causal, a, 0.0)
            vh = v_ref[0, rows, vcol]
            st = st_ref[h]
            o = jnp.dot(a.astype(BF16), vh, preferred_element_type=F32)
            o = o + lax.dot_general(qm, st.astype(BF16), _CONTRACT_LAST, preferred_element_type=F32)
            km = jnp.where(mine, k_end[rows, pair], zero)
            upd = lax.dot_general(vh, km, _CONTRACT_FIRST, preferred_element_type=F32)
            st_ref[h] = st * decay[ch * c_len:ch * c_len + 1, pair] + upd
            on = _rms(o, ng)
            gate = go_ref[0, rows, vcol].astype(F32)
            o_ref[0, rows, vcol] = (on * _silu(gate)).astype(BF16)


def _gla(big, small, w_gate, b_gate, norm_g):
    b, s, _ = big.shape
    hk = GLA_HEADS * GLA_DK
    hv = GLA_HEADS * GLA_DV
    wg = jnp.zeros((LANES, hk), F32).at[:GLA_GATE_RANK].set(w_gate).astype(BF16)
    blk = lambda n, cb: pl.BlockSpec((1, GLA_ROWS, n), lambda bi, si: (bi, si, cb))
    return pl.pallas_call(
        _gla_kernel,
        grid=(b, s // GLA_ROWS),
        in_specs=[blk(hk, 0), blk(hk, 1), blk(hv, 1), blk(hv, 2), blk(LANES, 0),
                  _resident(wg.shape), _resident((1, hk)), _resident((1, GLA_DV))],
        out_specs=blk(hv, 0),
        out_shape=jax.ShapeDtypeStruct((b, s, hv), BF16),
        scratch_shapes=[pltpu.VMEM((GLA_HEADS, GLA_DV, LANES), F32)],
        compiler_params=_params("parallel", "arbitrary"),
        name="gla",
    )(big, big, big, big, small, wg, b_gate.reshape(1, hk), norm_g.reshape(1, GLA_DV))


def _foxgate_kernel(fl_ref, b_ref, f_ref):
    x = _log_sigmoid(fl_ref[0] + b_ref[...])
    s = x.shape[1]
    lane = lax.broadcasted_iota(jnp.int32, x.shape, 1)
    sh = 1
    while sh < s:
        x = x + jnp.where(lane >= sh, pltpu.roll(x, sh, 1), 0.0)
        sh *= 2
    f_ref[0] = x


def _fox_gate(f_logit_t, bias):
    b, h, s = f_logit_t.shape
    blk = pl.BlockSpec((1, h, s), lambda bi: (bi, 0, 0))
    return pl.pallas_call(
        _foxgate_kernel,
        grid=(b,),
        in_specs=[blk, _resident((h, 1))],
        out_specs=blk,
        out_shape=jax.ShapeDtypeStruct((b, h, s), F32),
        compiler_params=_params("parallel"),
        name="fox_gate",
    )(f_logit_t, bias.reshape(h, 1))


def _fox_kernel(q_ref, k_ref, v_ref, f_ref, o_ref, qs_ref, mx_ref, l_ref, acc_ref):
    tq = q_ref.shape[1]
    tk = tq
    qi = pl.program_id(1)
    n_pairs = FOX_HEADS // 2
    lane = lax.broadcasted_iota(jnp.int32, (1, LANES), 1)
    zero = jnp.zeros((), BF16)
    for h in range(FOX_HEADS):
        q = q_ref[0, :, (h // 2) * LANES:(h // 2 + 1) * LANES] * jnp.asarray(FOX_DH ** -0.5, BF16)
        qs_ref[h * tq:(h + 1) * tq, :] = jnp.where((lane // FOX_DH) == (h % 2), q, zero)
    mx_ref[...] = jnp.full_like(mx_ref, NEG_BIG)
    l_ref[...] = jnp.zeros_like(l_ref)
    acc_ref[...] = jnp.zeros_like(acc_ref)
    row = lax.broadcasted_iota(jnp.int32, (tq, tk), 0)
    col = lax.broadcasted_iota(jnp.int32, (tq, tk), 1)

    def logits(j, diagonal):
        start = pl.multiple_of(j * tk, tk)
        for pair in range(n_pairs):
            kb = k_ref[0, pl.ds(start, tk), pair * LANES:(pair + 1) * LANES]
            s_all = lax.dot_general(qs_ref[2 * pair * tq:2 * (pair + 1) * tq, :], kb, _CONTRACT_LAST,
                                    preferred_element_type=F32)
            for hh in range(2):
                h = 2 * pair + hh
                s = s_all[hh * tq:(hh + 1) * tq] - f_ref[0, h, pl.ds(j, 1), :]
                if diagonal:
                    s = jnp.where(col <= row, s, NEG_BIG)
                yield slice(h * tq, (h + 1) * tq), s, pair

    def max_block(j, diagonal):
        for rows, s, _ in logits(j, diagonal):
            mx_ref[rows] = jnp.maximum(mx_ref[rows], s)

    def sum_block(j, diagonal):
        start = pl.multiple_of(j * tk, tk)
        for rows, s, pair in logits(j, diagonal):
            p = jnp.exp(s - mx_ref[rows])
            l_ref[rows] += p
            vb = v_ref[0, pl.ds(start, tk), pair * LANES:(pair + 1) * LANES]
            acc_ref[rows] += jnp.dot(p.astype(BF16), vb, preferred_element_type=F32)

    def loop(fn):
        def body(j, carry):
            fn(j, False)
            return carry
        lax.fori_loop(0, qi, body, 0)
        fn(qi, True)

    loop(max_block)
    for h in range(FOX_HEADS):
        rows = slice(h * tq, (h + 1) * tq)
        mx_ref[rows] = jnp.broadcast_to(jnp.max(mx_ref[rows], axis=-1, keepdims=True), (tq, tk))
    loop(sum_block)
    for pair in range(n_pairs):
        o = [acc_ref[h * tq:(h + 1) * tq] / jnp.sum(l_ref[h * tq:(h + 1) * tq], axis=-1, keepdims=True)
             for h in (2 * pair, 2 * pair + 1)]
        o_ref[0, :, pair * LANES:(pair + 1) * LANES] = jnp.where((lane // FOX_DH) == 0, o[0], o[1]).astype(BF16)


def _fox(big, f_cum, q_col, k_col, v_col):
    b, s, _ = big.shape
    hf = FOX_HEADS * FOX_DH
    nkb = s // FOX_TQ
    f4 = f_cum.reshape(b, FOX_HEADS, nkb, FOX_TQ)
    rows = FOX_HEADS * FOX_TQ
    return pl.pallas_call(
        _fox_kernel,
        grid=(b, s // FOX_TQ),
        in_specs=[
            pl.BlockSpec((1, FOX_TQ, hf), lambda bi, qi: (bi, qi, q_col)),
            pl.BlockSpec((1, s, hf), lambda bi, qi: (bi, 0, k_col)),
            pl.BlockSpec((1, s, hf), lambda bi, qi: (bi, 0, v_col)),
            pl.BlockSpec((1, FOX_HEADS, nkb, FOX_TQ), lambda bi, qi: (bi, 0, 0, 0)),
        ],
        out_specs=pl.BlockSpec((1, FOX_TQ, hf), lambda bi, qi: (bi, qi, 0)),
        out_shape=jax.ShapeDtypeStruct((b, s, hf), BF16),
        scratch_shapes=[pltpu.VMEM((rows, LANES), BF16),
                        pltpu.VMEM((rows, FOX_TQ), F32),
                        pltpu.VMEM((rows, FOX_TQ), F32),
                        pltpu.VMEM((rows, LANES), F32)],
        compiler_params=_params("parallel", "arbitrary"),
        name="fox",
    )(big, big, big, f4)


def _t5_bucket_np(dist):
    max_exact = T5_BUCKETS // 2
    d = np.maximum(dist, 1).astype(np.float32)
    large = max_exact + (np.log(d / np.float32(max_exact)) / np.float32(math.log(T5_MAX_DIST / max_exact))
                         * np.float32(T5_BUCKETS - max_exact)).astype(np.int32)
    large = np.minimum(large, T5_BUCKETS - 1)
    return np.where(dist < max_exact, dist, large).astype(np.int32)


N_BIAS_TILES = -(-(DSA_TK + T5_MAX_DIST - 1) // DSA_TQ)
LOG2E = math.log2(math.e)


def _bias_bucket_tiles():
    r = np.arange(DSA_TQ)[:, None]
    c = np.arange(DSA_TK)[None, :]
    return np.stack([_t5_bucket_np(np.maximum(t * DSA_TQ + r - c, 0)) for t in range(N_BIAS_TILES)])


def _bias_kernel(tab_ref, bkt_ref, o_ref):
    h = pl.program_id(1)
    bkt = bkt_ref[0]
    acc = jnp.zeros(bkt.shape, F32)
    for b in range(T5_BUCKETS):
        acc = jnp.where(bkt == b, tab_ref[b, h], acc)
    o_ref[0, 0] = (acc - tab_ref[T5_BUCKETS - 1, h]) * LOG2E


def _bias_tiles(t5_table):
    bkt = jnp.asarray(_bias_bucket_tiles())
    return pl.pallas_call(
        _bias_kernel,
        grid=(N_BIAS_TILES, DSA_HEADS),
        in_specs=[pl.BlockSpec(memory_space=pltpu.SMEM),
                  pl.BlockSpec((1, DSA_TQ, DSA_TK), lambda t, h: (t, 0, 0))],
        out_specs=pl.BlockSpec((1, 1, DSA_TQ, DSA_TK), lambda t, h: (t, h, 0, 0)),
        out_shape=jax.ShapeDtypeStruct((N_BIAS_TILES, DSA_HEADS, DSA_TQ, DSA_TK), F32),
        compiler_params=_params("parallel", "parallel"),
        name="t5_bias",
    )(t5_table, bkt)


def _dsa_kernel(q_ref, qi_ref, wt_ref, k2_ref, c_ref, wuk_ref, wuv_ref, bias_ref, o_ref,
                keyt_ref, hi_ref, lo_ref, selb_ref, qim_ref, qall_ref, p_ref, mx_ref, l_ref, acc_ref, *, k_top):
    tq, tk = DSA_TQ, DSA_TK
    sub = 8
    i = pl.program_id(1)
    nsb = (i * tq + tq + tk - 1) // tk
    lane = lax.broadcasted_iota(jnp.int32, (1, LANES), 1)
    qpos = i * tq + lax.broadcasted_iota(jnp.int32, (1, tq), 1)
    krow = lax.broadcasted_iota(jnp.int32, (tk, 1), 0)
    zero = jnp.zeros((), BF16)

    w_s = wt_ref[0] * ((IDX_HEADS ** -0.5) * (IDX_DIM ** -0.5))
    for h in range(IDX_HEADS):
        qp = qi_ref[0, :, (h // 2) * LANES:(h // 2 + 1) * LANES]
        qim_ref[h] = jnp.where((lane // IDX_DIM) == (h % 2), qp, zero)

    def score_block(j, carry):
        start = pl.multiple_of(j * tk, tk)
        kb = k2_ref[0, pl.ds(start, tk), :]
        sc = jnp.zeros((tk, tq), F32)
        for h in range(IDX_HEADS):
            lg = lax.dot_general(kb, qim_ref[h], _CONTRACT_LAST, preferred_element_type=F32)
            sc = sc + w_s[h:h + 1, :] * jnp.maximum(lg, 0.0)
        sc = jnp.where(sc == 0.0, 0.0, sc)
        bits = pltpu.bitcast(sc, jnp.int32)
        key = bits ^ ((bits >> 31) & 0x7FFFFFFF)
        key = jnp.where(start + krow <= qpos, key, KEY_NEG_INF)
        keyt_ref[j] = key
        hi_ref[j] = (key >> 16).astype(jnp.int16)
        lo_ref[j] = ((key & 0xFFFF) - HALF).astype(jnp.int16)
        return carry

    lax.fori_loop(0, nsb, score_block, 0)
    lowest = jnp.full((tk, tq), -HALF, jnp.int16)
    hi_ref[nsb] = lowest
    lo_ref[nsb] = lowest

    def tree_sum(x, rows):
        parts = [x[r * rows:(r + 1) * rows] for r in range(x.shape[0] // rows)]
        while len(parts) > 1:
            parts = [a + b for a, b in zip(parts[::2], parts[1::2])]
        return parts[0]

    def count(pred):
        def body(j, acc):
            return acc + tree_sum(jnp.where(pred(keyt_ref[j], j), 1.0, 0.0), sub)
        per_sublane = lax.fori_loop(0, nsb, body, jnp.zeros((sub, tq), F32))
        return jnp.sum(per_sublane, axis=0, keepdims=True)

    def count16(ref, pred):
        one, none = jnp.ones((), jnp.int16), jnp.zeros((), jnp.int16)

        def body(t, acc):
            x = jnp.where(pred(ref[2 * t]), one, none) + jnp.where(pred(ref[2 * t + 1]), one, none)
            return acc + tree_sum(x, 2 * sub)
        per_row = lax.fori_loop(0, (nsb + 1) // 2, body, jnp.zeros((2 * sub, tq), jnp.int16))
        return jnp.sum(per_row.astype(jnp.int32), axis=0, keepdims=True)

    def search16(ref, want):
        def step(t, u):
            cand_u = u | lax.shift_left(jnp.int32(1), 15 - t)
            cand = (cand_u - HALF).astype(jnp.int16)
            n = count16(ref, lambda v: v >= cand)
            return jnp.where(n >= want, cand_u, u)
        return lax.fori_loop(0, 16, step, jnp.zeros((1, tq), jnp.int32)) - HALF

    t_hi = search16(hi_ref, k_top)
    t_hi16 = t_hi.astype(jnp.int16)
    n_above = count16(hi_ref, lambda v: v > t_hi16)

    def keep_ties(j, carry):
        lo_ref[j] = jnp.where(hi_ref[j] == t_hi16, lo_ref[j], jnp.asarray(-HALF, jnp.int16))
        return carry

    lax.fori_loop(0, nsb, keep_ties, 0)
    t_lo = search16(lo_ref, k_top - n_above)
    thr = lax.shift_left(t_hi, 16) | (t_lo + HALF)
    n_gt = count(lambda kk, j: kk > thr)
    n_ge = count(lambda kk, j: kk >= thr)
    need = k_top - n_gt
    excess = (n_ge > k_top) & (thr > KEY_NEG_INF)
    s_total = k2_ref.shape[1]

    def tie_search():
        def step(t, p):
            cand = p | lax.shift_left(jnp.int32(1), (s_total.bit_length() - 1) - t)
            n = count(lambda kk, j: (kk == thr) & (j * tk + krow <= cand - 1))
            return jnp.where(n < need, cand, p)
        p = lax.fori_loop(0, s_total.bit_length(), step, jnp.zeros((1, tq), jnp.int32))
        return jnp.where(excess, p, s_total)

    any_excess = jnp.max(jnp.where(excess, 1.0, 0.0)) > 0.0
    tie_limit = lax.cond(any_excess, tie_search, lambda: jnp.full((1, tq), s_total, jnp.int32))

    def select_block(j, carry):
        key = keyt_ref[j]
        kpos = j * tk + krow
        sel = ((key > thr) | ((key == thr) & (kpos <= tie_limit))) & (kpos <= qpos)
        selb_ref[j] = jnp.where(sel, 0.0, NEG_BIG).T
        return carry

    lax.fori_loop(0, nsb, select_block, 0)

    for h in range(DSA_HEADS):
        qp = q_ref[0, :, (h // 2) * LANES:(h // 2 + 1) * LANES]
        ql = jnp.dot(qp, wuk_ref[h], preferred_element_type=F32) * ((DSA_DH ** -0.5) * LOG2E)
        qall_ref[h * tq:(h + 1) * tq, :] = ql.astype(BF16)
    mx_ref[...] = jnp.full_like(mx_ref, NEG_BIG)
    l_ref[...] = jnp.zeros_like(l_ref)
    acc_ref[...] = jnp.zeros_like(acc_ref)
    n_far = jnp.maximum(((i - N_BIAS_TILES) * tq) // tk + 1, 0)

    def logits(j0, nblk, near):
        start = pl.multiple_of(j0 * tk, tk)
        cb = c_ref[0, pl.ds(start, nblk * tk), :]
        s_all = lax.dot_general(qall_ref[...], cb, _CONTRACT_LAST, preferred_element_type=F32)
        for h in range(DSA_HEADS):
            rows = slice(h * tq, (h + 1) * tq)
            tiles = []
            for b in range(nblk):
                s = s_all[rows, b * tk:(b + 1) * tk] + selb_ref[j0 + b]
                if near:
                    s = s + bias_ref[(i * tq - (j0 + b) * tk) // tq, h]
                tiles.append(s)
            yield rows, tiles, cb

    def max_blocks(j0, nblk, near):
        for rows, tiles, _ in logits(j0, nblk, near):
            m = mx_ref[rows]
            for s in tiles:
                m = jnp.maximum(m, s)
            mx_ref[rows] = m

    def sum_blocks(j0, nblk, near):
        for rows, tiles, cb in logits(j0, nblk, near):
            m = mx_ref[rows]
            l = l_ref[rows]
            for b, s in enumerate(tiles):
                p = jnp.exp2(s - m)
                l = l + p
                p_ref[rows, b * tk:(b + 1) * tk] = p.astype(BF16)
            l_ref[rows] = l
        acc_ref[...] += jnp.dot(p_ref[:, :nblk * tk], cb, preferred_element_type=F32)

    def sweep(fn):
        def loop(lo, hi, stride, near):
            def body(t, carry):
                fn(lo + t * stride, stride, near)
                return carry
            lax.fori_loop(0, (hi - lo) // stride, body, 0)
        n_even = (n_far // 2) * 2
        loop(0, n_even, 2, False)
        loop(n_even, n_far, 1, False)
        loop(n_far, nsb, 1, True)

    sweep(max_blocks)
    for h in range(DSA_HEADS):
        rows = slice(h * tq, (h + 1) * tq)
        mx_ref[rows] = jnp.broadcast_to(jnp.max(mx_ref[rows], axis=-1, keepdims=True), (tq, tk))
    sweep(sum_blocks)

    for pair in range(DSA_HEADS // 2):
        out = jnp.zeros((tq, LANES), F32)
        for h in (2 * pair, 2 * pair + 1):
            rows = slice(h * tq, (h + 1) * tq)
            o_lat = (acc_ref[rows] / jnp.sum(l_ref[rows], axis=-1, keepdims=True)).astype(BF16)
            out = out + jnp.dot(o_lat, wuv_ref[h], preferred_element_type=F32)
        o_ref[0, :, pair * LANES:(pair + 1) * LANES] = out.astype(BF16)


def _dsa(big, w_idx_t, c, w_uk, w_uv, bias_tiles, q_col, qi_col, k2_col):
    b, s, _ = big.shape
    k_top = min(TOPK_MAX, s // 4)
    hq = DSA_HEADS * DSA_DH
    hi = IDX_HEADS * IDX_DIM
    wuk = jnp.zeros((DSA_HEADS, LANES, DSA_LATENT), F32)
    wuv = jnp.zeros((DSA_HEADS, DSA_LATENT, LANES), F32)
    even = np.arange(0, DSA_HEADS, 2)
    wuk = wuk.at[even, :DSA_DH].set(w_uk[even]).at[even + 1, DSA_DH:].set(w_uk[even + 1]).astype(BF16)
    wuv = wuv.at[even, :, :DSA_DH].set(w_uv[even]).at[even + 1, :, DSA_DH:].set(w_uv[even + 1]).astype(BF16)
    nq = s // DSA_TQ
    rows = DSA_HEADS * DSA_TQ
    return pl.pallas_call(
        functools.partial(_dsa_kernel, k_top=k_top),
        grid=(b, nq),
        in_specs=[
            pl.BlockSpec((1, DSA_TQ, hq), lambda bi, qi: (bi, qi, q_col)),
            pl.BlockSpec((1, DSA_TQ, hi), lambda bi, qi: (bi, qi, qi_col)),
            pl.BlockSpec((1, IDX_HEADS, DSA_TQ), lambda bi, qi: (bi, 0, qi)),
            pl.BlockSpec((1, s, LANES), lambda bi, qi: (bi, 0, k2_col)),
            pl.BlockSpec((1, s, DSA_LATENT), lambda bi, qi: (bi, 0, 0)),
            _resident(wuk.shape), _resident(wuv.shape), _resident(bias_tiles.shape),
        ],
        out_specs=pl.BlockSpec((1, DSA_TQ, hq), lambda bi, qi: (bi, qi, 0)),
        out_shape=jax.ShapeDtypeStruct((b, s, hq), BF16),
        scratch_shapes=[
            pltpu.VMEM((s // DSA_TK, DSA_TK, DSA_TQ), jnp.int32),
            pltpu.VMEM((s // DSA_TK + 1, DSA_TK, DSA_TQ), jnp.int16),
            pltpu.VMEM((s // DSA_TK + 1, DSA_TK, DSA_TQ), jnp.int16),
            pltpu.VMEM((s // DSA_TK, DSA_TQ, DSA_TK), F32),
            pltpu.VMEM((IDX_HEADS, DSA_TQ, LANES), BF16),
            pltpu.VMEM((rows, DSA_LATENT), BF16),
            pltpu.VMEM((rows, 2 * DSA_TK), BF16),
            pltpu.VMEM((rows, DSA_TK), F32),
            pltpu.VMEM((rows, DSA_TK), F32),
            pltpu.VMEM((rows, DSA_LATENT), F32),
        ],
        compiler_params=_params("parallel", "arbitrary"),
        name="dsa",
    )(big, big, w_idx_t, big, c, wuk, wuv, bias_tiles)


def _even_mixer(x2, b, s, g, w_in, w_gate, b_gate, norm_g, fox_b, w_out):
    hk = GLA_HEADS * GLA_DK
    hv = GLA_HEADS * GLA_DV
    hf = FOX_HEADS * FOX_DH
    o = np.cumsum([0, hk, hk, hv, hv, GLA_GATE_RANK, hf, hf, hf, FOX_HEADS])
    col = lambda n: w_in[:, o[n]:o[n + 1]]
    w_big = jnp.concatenate([col(0), col(1), col(2), col(3), col(5), col(6), col(7)], axis=1)
    w_small = jnp.zeros((w_in.shape[0], LANES), F32)
    w_small = w_small.at[:, :GLA_GATE_RANK].set(col(4)).at[:, GLA_GATE_RANK:GLA_GATE_RANK + FOX_HEADS].set(col(8))
    big, small = _inproj(x2, g, w_big, w_small)
    big = big.reshape(b, s, -1)
    small = small.reshape(b, s, LANES)
    o_gla = _gla(big, small, w_gate, b_gate, norm_g)
    f_logit_t = small[:, :, GLA_GATE_RANK:GLA_GATE_RANK + FOX_HEADS].transpose(0, 2, 1)
    f_cum = _fox_gate(f_logit_t, fox_b)
    fox_col = (2 * hk + 2 * hv) // hf
    o_fox = _fox(big, f_cum, fox_col, fox_col + 1, fox_col + 2)
    return _outproj(x2, [o_gla.reshape(b * s, hv), o_fox.reshape(b * s, hf)], w_out)


def _odd_mixer(x2, b, s, g, w_in, kv_g, w_uk, w_uv, w_out, bias_tiles):
    hq = DSA_HEADS * DSA_DH
    hi = IDX_HEADS * IDX_DIM
    o = np.cumsum([0, hq, DSA_LATENT, hi, IDX_DIM, IDX_HEADS])
    col = lambda n: w_in[:, o[n]:o[n + 1]]
    w_big = jnp.concatenate([col(0), col(2), col(3), col(3)], axis=1)
    w_small = jnp.zeros((w_in.shape[0], LANES), F32).at[:, :IDX_HEADS].set(col(4))
    big, small, c = _inproj(x2, g, w_big, w_small, col(1), kv_g)
    big = big.reshape(b, s, -1)
    w_idx_t = small.reshape(b, s, LANES)[:, :, :IDX_HEADS].transpose(0, 2, 1)
    o_dsa = _dsa(big, w_idx_t, c.reshape(b, s, DSA_LATENT), w_uk, w_uv, bias_tiles,
                 0, hq // hi, (hq + hi) // LANES)
    return _outproj(x2, [o_dsa.reshape(b * s, hq)], w_out)


def kernel(x, norm_g, ffn_w_in, ffn_w_out, even_w_in, gla_w_gate, gla_b_gate, gla_norm_g, fox_b_f, even_w_out,
           odd_w_in, mla_kv_norm_g, mla_w_uk, mla_w_uv, odd_w_out, t5_table, final_norm_g):
    b, s, d = x.shape
    depth = norm_g.shape[0]
    x2 = x.reshape(b * s, d)
    bias_tiles = _bias_tiles(t5_table) if depth > 1 else None
    for layer in range(depth):
        g = norm_g[layer]
        j = layer // 2
        x2 = _ffn(x2, g[0], ffn_w_in[layer, 0], ffn_w_out[layer, 0])
        if layer % 2 == 0:
            x2 = _even_mixer(x2, b, s, g[1], even_w_in[j], gla_w_gate[j], gla_b_gate[j], gla_norm_g[j],
                             fox_b_f[j], even_w_out[j])
        else:
            x2 = _odd_mixer(x2, b, s, g[1], odd_w_in[j], mla_kv_norm_g[j], mla_w_uk[j], mla_w_uv[j],
                            odd_w_out[j], bias_tiles)
        last = layer == depth - 1
        x2 = _ffn(x2, g[2], ffn_w_in[layer, 1], ffn_w_out[layer, 1], final_norm_g if last else None)
    return x2.reshape(b, s, d)
```

```python
import functools
import math

import numpy as np
import jax
import jax.numpy as jnp
from jax import lax
from jax.experimental import pallas as pl
from jax.experimental.pallas import tpu as pltpu

F32 = jnp.float32
BF16 = jnp.bfloat16

EPS = 1e-6
GLA_HEADS = 4
GLA_DK = 64
GLA_DV = 128
GLA_GATE_RANK = 16
GLA_GATE_NORMALIZER = 16.0
GLA_CHUNK = 64
FOX_HEADS = 8
FOX_DH = 64
DSA_HEADS = 16
DSA_DH = 64
DSA_LATENT = 256
IDX_HEADS = 8
IDX_DIM = 64
TOPK_MAX = 256
T5_BUCKETS = 32
T5_MAX_DIST = 128

LANES = 128
VMEM_LIMIT = 56 * 1024 * 1024

ROW_TILE = 512
FFN_ROW_TILE = 1024
FFN_CHUNK = 256
GLA_ROWS = 256
FOX_TQ = 512
DSA_TQ = 128
DSA_TK = 256
FAR_STEP = 4
NEG_BIG = -1e30
UNDERFLOW_GUARD = 2.0 ** -60
INT_MIN = -2 ** 31
KEY_NEG_INF = INT_MIN + 0x7FFFFF
HALF = 1 << 15

_CONTRACT_LAST = (((1,), (1,)), ((), ()))
_CONTRACT_FIRST = (((0,), (0,)), ((), ()))


def _params(*sem):
    return pltpu.CompilerParams(dimension_semantics=sem, vmem_limit_bytes=VMEM_LIMIT)


def _resident(shape):
    nd = len(shape)
    return pl.BlockSpec(shape, lambda *_: (0,) * nd, pipeline_mode=pl.Buffered(1))


def _rms(x, g):
    return x * lax.rsqrt(jnp.mean(x * x, axis=-1, keepdims=True) + EPS) * g


def _log_sigmoid(x):
    return jnp.minimum(x, 0.0) - jnp.log1p(jnp.exp(-jnp.abs(x)))


def _silu(x):
    return x * jax.nn.sigmoid(x)


def _ffn_kernel(*refs, n_chunks, final):
    if final:
        x_ref, g_ref, wi_ref, wo_ref, gf_ref, o_ref, h_ref, acc_ref = refs
    else:
        x_ref, g_ref, wi_ref, wo_ref, o_ref, h_ref, acc_ref = refs
    x = x_ref[...]
    h_ref[...] = _rms(x, g_ref[...]).astype(BF16)
    f = wo_ref.shape[0]
    for j in range(n_chunks):
        lo, hi = j * FFN_CHUNK, (j + 1) * FFN_CHUNK
        h = h_ref[...]
        a = jnp.dot(h, wi_ref[:, lo:hi], preferred_element_type=F32)
        b = jnp.dot(h, wi_ref[:, f + lo:f + hi], preferred_element_type=F32)
        act = (_silu(a) * b).astype(BF16)
        part = jnp.dot(act, wo_ref[lo:hi, :], preferred_element_type=F32)
        if j == 0:
            acc_ref[...] = part
        else:
            acc_ref[...] += part
    y = x + 0.5 * acc_ref[...]
    if final:
        y = _rms(y, gf_ref[...])
    o_ref[...] = y


def _ffn(x2, g, w_in, w_out, final_g=None):
    t, d = x2.shape
    f = w_out.shape[0]
    n_chunks = f // FFN_CHUNK
    wi = w_in.astype(BF16)
    wo = w_out.astype(BF16)
    final = final_g is not None
    row = pl.BlockSpec((FFN_ROW_TILE, d), lambda i: (i, 0))
    in_specs = [row, _resident((1, d)), _resident(wi.shape), _resident(wo.shape)]
    args = [x2, g.reshape(1, d), wi, wo]
    if final:
        in_specs.append(_resident((1, d)))
        args.append(final_g.reshape(1, d))
    return pl.pallas_call(
        functools.partial(_ffn_kernel, n_chunks=n_chunks, final=final),
        grid=(t // FFN_ROW_TILE,),
        in_specs=in_specs,
        out_specs=row,
        out_shape=jax.ShapeDtypeStruct((t, d), F32),
        scratch_shapes=[pltpu.VMEM((FFN_ROW_TILE, d), BF16), pltpu.VMEM((FFN_ROW_TILE, d), F32)],
        compiler_params=_params("parallel"),
        name="ffn",
    )(*args)


def _inproj_kernel(*refs, big_chunks, with_c):
    if with_c:
        x_ref, g_ref, wbig_ref, wsm_ref, wc_ref, gc_ref, big_ref, sm_ref, c_ref = refs
    else:
        x_ref, g_ref, wbig_ref, wsm_ref, big_ref, sm_ref = refs
    h = _rms(x_ref[...], g_ref[...]).astype(BF16)
    for lo, hi in big_chunks:
        big_ref[:, lo:hi] = jnp.dot(h, wbig_ref[:, lo:hi], preferred_element_type=F32).astype(BF16)
    sm_ref[...] = jnp.dot(h, wsm_ref[...], preferred_element_type=F32)
    if with_c:
        ckv = jnp.dot(h, wc_ref[...], preferred_element_type=F32)
        c_ref[...] = _rms(ckv, gc_ref[...]).astype(BF16)


def _inproj(x2, g, w_big, w_small, w_c=None, g_c=None):
    t, d = x2.shape
    nbig = w_big.shape[1]
    with_c = w_c is not None
    step = 4 * LANES
    big_chunks = tuple((lo, min(lo + step, nbig)) for lo in range(0, nbig, step))
    row = lambda n: pl.BlockSpec((ROW_TILE, n), lambda i: (i, 0))
    in_specs = [row(d), _resident((1, d)), _resident(w_big.shape), _resident(w_small.shape)]
    args = [x2, g.reshape(1, d), w_big.astype(BF16), w_small.astype(BF16)]
    out_specs = [row(nbig), row(LANES)]
    out_shape = [jax.ShapeDtypeStruct((t, nbig), BF16), jax.ShapeDtypeStruct((t, LANES), F32)]
    if with_c:
        nc = w_c.shape[1]
        in_specs += [_resident(w_c.shape), _resident((1, nc))]
        args += [w_c.astype(BF16), g_c.reshape(1, nc)]
        out_specs.append(row(nc))
        out_shape.append(jax.ShapeDtypeStruct((t, nc), BF16))
    return pl.pallas_call(
        functools.partial(_inproj_kernel, big_chunks=big_chunks, with_c=with_c),
        grid=(t // ROW_TILE,),
        in_specs=in_specs,
        out_specs=out_specs,
        out_shape=out_shape,
        compiler_params=_params("parallel"),
        name="inproj_c" if with_c else "inproj",
    )(*args)


def _outproj_kernel(*refs, n_in):
    x_ref = refs[0]
    o_refs = refs[1:1 + n_in]
    w_refs = refs[1 + n_in:1 + 2 * n_in]
    y_ref = refs[1 + 2 * n_in]
    y = x_ref[...]
    for o_ref, w_ref in zip(o_refs, w_refs):
        y = y + jnp.dot(o_ref[...], w_ref[...], preferred_element_type=F32)
    y_ref[...] = y


def _outproj(x2, outs, w_out):
    t, d = x2.shape
    ws, lo = [], 0
    for o in outs:
        ws.append(w_out[lo:lo + o.shape[1]].astype(BF16))
        lo += o.shape[1]
    row = lambda n: pl.BlockSpec((ROW_TILE, n), lambda i: (i, 0))
    return pl.pallas_call(
        functools.partial(_outproj_kernel, n_in=len(outs)),
        grid=(t // ROW_TILE,),
        in_specs=[row(d)] + [row(o.shape[1]) for o in outs] + [_resident(w.shape) for w in ws],
        out_specs=row(d),
        out_shape=jax.ShapeDtypeStruct((t, d), F32),
        compiler_params=_params("parallel"),
        name="outproj",
    )(x2, *outs, *ws)


def _gla_kernel(q_ref, k_ref, v_ref, go_ref, sm_ref, wg_ref, bg_ref, ng_ref, o_ref, st_ref):
    rs = q_ref.shape[1]
    c_len = GLA_CHUNK

    @pl.when(pl.program_id(1) == 0)
    def _():
        st_ref[...] = jnp.zeros_like(st_ref)

    pre = jnp.dot(sm_ref[0].astype(BF16), wg_ref[...], preferred_element_type=F32) + bg_ref[...]
    gk = _log_sigmoid(pre) / GLA_GATE_NORMALIZER
    r = lax.broadcasted_iota(jnp.int32, (rs, rs), 0)
    c = lax.broadcasted_iota(jnp.int32, (rs, rs), 1)
    same = (r // c_len) == (c // c_len)
    tri = jnp.where(same & (c <= r), 1.0, 0.0).astype(F32)
    ones = jnp.where(same, 1.0, 0.0).astype(F32)
    g_cum = jnp.dot(tri, gk, precision=lax.Precision.HIGHEST, preferred_element_type=F32)
    g_last = jnp.dot(ones, gk, precision=lax.Precision.HIGHEST, preferred_element_type=F32)
    q = q_ref[0].astype(F32) * (GLA_DK ** -0.5)
    k = k_ref[0].astype(F32)
    q_dec = (q * jnp.exp(g_cum)).astype(BF16)
    k_dec = (k * jnp.exp(-g_cum)).astype(BF16)
    k_end = (k * jnp.exp(g_last - g_cum)).astype(BF16)
    decay = jnp.exp(g_last)

    lane = lax.broadcasted_iota(jnp.int32, (1, LANES), 1)
    ri = lax.broadcasted_iota(jnp.int32, (c_len, c_len), 0)
    ci = lax.broadcasted_iota(jnp.int32, (c_len, c_len), 1)
    causal = ci <= ri
    ng = ng_ref[...]
    zero = jnp.zeros((), BF16)
    for ch in range(rs // c_len):
        rows = slice(ch * c_len, (ch + 1) * c_len)
        for h in range(GLA_HEADS):
            pair = slice((h // 2) * LANES, (h // 2 + 1) * LANES)
            mine = (lane // GLA_DK) == (h % 2)
            vcol = slice(h * GLA_DV, (h + 1) * GLA_DV)
            qm = jnp.where(mine, q_dec[rows, pair], zero)
            a = lax.dot_general(qm, k_dec[rows, pair], _CONTRACT_LAST, preferred_element_type=F32)
            a = jnp.where(causal, a, 0.0)
            vh = v_ref[0, rows, vcol]
            st = st_ref[h]
            o = jnp.dot(a.astype(BF16), vh, preferred_element_type=F32)
            o = o + lax.dot_general(qm, st.astype(BF16), _CONTRACT_LAST, preferred_element_type=F32)
            km = jnp.where(mine, k_end[rows, pair], zero)
            upd = lax.dot_general(vh, km, _CONTRACT_FIRST, preferred_element_type=F32)
            st_ref[h] = st * decay[ch * c_len:ch * c_len + 1, pair] + upd
            on = _rms(o, ng)
            gate = go_ref[0, rows, vcol].astype(F32)
            o_ref[0, rows, vcol] = (on * _silu(gate)).astype(BF16)


def _gla(big, small, w_gate, b_gate, norm_g):
    b, s, _ = big.shape
    hk = GLA_HEADS * GLA_DK
    hv = GLA_HEADS * GLA_DV
    wg = jnp.zeros((LANES, hk), F32).at[:GLA_GATE_RANK].set(w_gate).astype(BF16)
    blk = lambda n, cb: pl.BlockSpec((1, GLA_ROWS, n), lambda bi, si: (bi, si, cb))
    return pl.pallas_call(
        _gla_kernel,
        grid=(b, s // GLA_ROWS),
        in_specs=[blk(hk, 0), blk(hk, 1), blk(hv, 1), blk(hv, 2), blk(LANES, 0),
                  _resident(wg.shape), _resident((1, hk)), _resident((1, GLA_DV))],
        out_specs=blk(hv, 0),
        out_shape=jax.ShapeDtypeStruct((b, s, hv), BF16),
        scratch_shapes=[pltpu.VMEM((GLA_HEADS, GLA_DV, LANES), F32)],
        compiler_params=_params("parallel", "arbitrary"),
        name="gla",
    )(big, big, big, big, small, wg, b_gate.reshape(1, hk), norm_g.reshape(1, GLA_DV))


def _foxgate_kernel(fl_ref, b_ref, f_ref):
    x = _log_sigmoid(fl_ref[0] + b_ref[...])
    s = x.shape[1]
    lane = lax.broadcasted_iota(jnp.int32, x.shape, 1)
    sh = 1
    while sh < s:
        x = x + jnp.where(lane >= sh, pltpu.roll(x, sh, 1), 0.0)
        sh *= 2
    f_ref[0] = x


def _fox_gate(f_logit_t, bias):
    b, h, s = f_logit_t.shape
    blk = pl.BlockSpec((1, h, s), lambda bi: (bi, 0, 0))
    return pl.pallas_call(
        _foxgate_kernel,
        grid=(b,),
        in_specs=[blk, _resident((h, 1))],
        out_specs=blk,
        out_shape=jax.ShapeDtypeStruct((b, h, s), F32),
        compiler_params=_params("parallel"),
        name="fox_gate",
    )(f_logit_t, bias.reshape(h, 1))


def _fox_kernel(q_ref, k_ref, v_ref, f_ref, fcol_ref, o_ref, qs_ref, mx_ref, l_ref, acc_ref, lsum_ref, kmax_ref):
    tq = q_ref.shape[1]
    tk = tq
    qi = pl.program_id(1)
    n_pairs = FOX_HEADS // 2
    lane = lax.broadcasted_iota(jnp.int32, (1, LANES), 1)
    zero = jnp.zeros((), BF16)

    @pl.when(qi == 0)
    def _():
        for pair in range(n_pairs):
            kf = k_ref[0, :, pair * LANES:(pair + 1) * LANES].astype(F32)
            kmax_ref[pair] = jnp.full(kmax_ref.shape[1:], jnp.sqrt(jnp.max(jnp.sum(kf * kf, axis=-1, keepdims=True))))

    f_rows = fcol_ref[0]
    for h in range(FOX_HEADS):
        rows = slice(h * tq, (h + 1) * tq)
        q = q_ref[0, :, (h // 2) * LANES:(h // 2 + 1) * LANES] * jnp.asarray(FOX_DH ** -0.5, BF16)
        q = jnp.where((lane // FOX_DH) == (h % 2), q, zero)
        qs_ref[rows, :] = q
        qf = q.astype(F32)
        bound = jnp.sqrt(jnp.sum(qf * qf, axis=-1, keepdims=True)) * kmax_ref[h // 2, 0:1, 0:1] - f_rows[:, h:h + 1]
        mx_ref[rows] = jnp.broadcast_to(bound, (tq, tk))
    l_ref[...] = jnp.zeros_like(l_ref)
    acc_ref[...] = jnp.zeros_like(acc_ref)
    row = lax.broadcasted_iota(jnp.int32, (tq, tk), 0)
    col = lax.broadcasted_iota(jnp.int32, (tq, tk), 1)

    def logits(j, diagonal):
        start = pl.multiple_of(j * tk, tk)
        for pair in range(n_pairs):
            kb = k_ref[0, pl.ds(start, tk), pair * LANES:(pair + 1) * LANES]
            s_all = lax.dot_general(qs_ref[2 * pair * tq:2 * (pair + 1) * tq, :], kb, _CONTRACT_LAST,
                                    preferred_element_type=F32)
            for hh in range(2):
                h = 2 * pair + hh
                s = s_all[hh * tq:(hh + 1) * tq] - f_ref[0, h, pl.ds(j, 1), :]
                if diagonal:
                    s = jnp.where(col <= row, s, NEG_BIG)
                yield slice(h * tq, (h + 1) * tq), s, pair

    def max_block(j, diagonal):
        for rows, s, _ in logits(j, diagonal):
            mx_ref[rows] = jnp.maximum(mx_ref[rows], s)

    def sum_block(j, diagonal):
        start = pl.multiple_of(j * tk, tk)
        for rows, s, pair in logits(j, diagonal):
            p = jnp.exp(s - mx_ref[rows])
            l_ref[rows] += p
            vb = v_ref[0, pl.ds(start, tk), pair * LANES:(pair + 1) * LANES]
            acc_ref[rows] += jnp.dot(p.astype(BF16), vb, preferred_element_type=F32)

    def loop(fn):
        def body(j, carry):
            fn(j, False)
            return carry
        lax.fori_loop(0, qi, body, 0)
        fn(qi, True)

    def row_sums():
        smallest = None
        for h in range(FOX_HEADS):
            rows = slice(h * tq, (h + 1) * tq)
            ls = jnp.sum(l_ref[rows], axis=-1, keepdims=True)
            lsum_ref[rows] = ls
            smallest = ls if smallest is None else jnp.minimum(smallest, ls)
        return jnp.min(smallest)

    loop(sum_block)

    @pl.when(row_sums() < UNDERFLOW_GUARD)
    def _():
        mx_ref[...] = jnp.full_like(mx_ref, NEG_BIG)
        l_ref[...] = jnp.zeros_like(l_ref)
        acc_ref[...] = jnp.zeros_like(acc_ref)
        loop(max_block)
        for h in range(FOX_HEADS):
            rows = slice(h * tq, (h + 1) * tq)
            mx_ref[rows] = jnp.broadcast_to(jnp.max(mx_ref[rows], axis=-1, keepdims=True), (tq, tk))
        loop(sum_block)
        row_sums()

    for pair in range(n_pairs):
        o = [acc_ref[h * tq:(h + 1) * tq] / lsum_ref[h * tq:(h + 1) * tq] for h in (2 * pair, 2 * pair + 1)]
        o_ref[0, :, pair * LANES:(pair + 1) * LANES] = jnp.where((lane // FOX_DH) == 0, o[0], o[1]).astype(BF16)


def _fox(big, f_cum, q_col, k_col, v_col):
    b, s, _ = big.shape
    hf = FOX_HEADS * FOX_DH
    nkb = s // FOX_TQ
    f4 = f_cum.reshape(b, FOX_HEADS, nkb, FOX_TQ)
    rows = FOX_HEADS * FOX_TQ
    return pl.pallas_call(
        _fox_kernel,
        grid=(b, s // FOX_TQ),
        in_specs=[
            pl.BlockSpec((1, FOX_TQ, hf), lambda bi, qi: (bi, qi, q_col)),
            pl.BlockSpec((1, s, hf), lambda bi, qi: (bi, 0, k_col)),
            pl.BlockSpec((1, s, hf), lambda bi, qi: (bi, 0, v_col)),
            pl.BlockSpec((1, FOX_HEADS, nkb, FOX_TQ), lambda bi, qi: (bi, 0, 0, 0)),
            pl.BlockSpec((1, FOX_TQ, FOX_HEADS), lambda bi, qi: (bi, qi, 0)),
        ],
        out_specs=pl.BlockSpec((1, FOX_TQ, hf), lambda bi, qi: (bi, qi, 0)),
        out_shape=jax.ShapeDtypeStruct((b, s, hf), BF16),
        scratch_shapes=[pltpu.VMEM((rows, LANES), BF16),
                        pltpu.VMEM((rows, FOX_TQ), F32),
                        pltpu.VMEM((rows, FOX_TQ), F32),
                        pltpu.VMEM((rows, LANES), F32),
                        pltpu.VMEM((rows, 1), F32),
                        pltpu.VMEM((FOX_HEADS // 2, 8, LANES), F32)],
        compiler_params=_params("parallel", "arbitrary"),
        name="fox",
    )(big, big, big, f4, f_cum.transpose(0, 2, 1))


def _t5_bucket_np(dist):
    max_exact = T5_BUCKETS // 2
    d = np.maximum(dist, 1).astype(np.float32)
    large = max_exact + (np.log(d / np.float32(max_exact)) / np.float32(math.log(T5_MAX_DIST / max_exact))
                         * np.float32(T5_BUCKETS - max_exact)).astype(np.int32)
    large = np.minimum(large, T5_BUCKETS - 1)
    return np.where(dist < max_exact, dist, large).astype(np.int32)


N_BIAS_TILES = -(-(DSA_TK + T5_MAX_DIST - 1) // DSA_TQ)
LOG2E = math.log2(math.e)


def _bias_bucket_tiles():
    r = np.arange(DSA_TQ)[:, None]
    c = np.arange(DSA_TK)[None, :]
    return np.stack([_t5_bucket_np(np.maximum(t * DSA_TQ + r - c, 0)) for t in range(N_BIAS_TILES)])


def _bias_kernel(tab_ref, bkt_ref, o_ref):
    h = pl.program_id(1)
    bkt = bkt_ref[0]
    acc = jnp.zeros(bkt.shape, F32)
    for b in range(T5_BUCKETS):
        acc = jnp.where(bkt == b, tab_ref[b, h], acc)
    o_ref[0, 0] = (acc - tab_ref[T5_BUCKETS - 1, h]) * LOG2E


def _bias_tiles(t5_table):
    bkt = jnp.asarray(_bias_bucket_tiles())
    return pl.pallas_call(
        _bias_kernel,
        grid=(N_BIAS_TILES, DSA_HEADS),
        in_specs=[pl.BlockSpec(memory_space=pltpu.SMEM),
                  pl.BlockSpec((1, DSA_TQ, DSA_TK), lambda t, h: (t, 0, 0))],
        out_specs=pl.BlockSpec((1, 1, DSA_TQ, DSA_TK), lambda t, h: (t, h, 0, 0)),
        out_shape=jax.ShapeDtypeStruct((N_BIAS_TILES, DSA_HEADS, DSA_TQ, DSA_TK), F32),
        compiler_params=_params("parallel", "parallel"),
        name="t5_bias",
    )(t5_table, bkt)


def _dsa_kernel(q_ref, qi_ref, wt_ref, k2_ref, c_ref, wuk_ref, wuv_ref, bias_ref, tab_ref, o_ref,
                keyt_ref, hi_ref, lo_ref, selb_ref, qim_ref, qall_ref, p_ref, mx_ref, l_ref, acc_ref,
                lsum_ref, cmax_ref, *, k_top):
    tq, tk = DSA_TQ, DSA_TK
    sub = 8
    i = pl.program_id(1)
    nsb = (i * tq + tq + tk - 1) // tk
    lane = lax.broadcasted_iota(jnp.int32, (1, LANES), 1)
    qpos = i * tq + lax.broadcasted_iota(jnp.int32, (1, tq), 1)
    krow = lax.broadcasted_iota(jnp.int32, (tk, 1), 0)
    zero = jnp.zeros((), BF16)

    w_s = wt_ref[0] * ((IDX_HEADS ** -0.5) * (IDX_DIM ** -0.5))
    for h in range(IDX_HEADS):
        qp = qi_ref[0, :, (h // 2) * LANES:(h // 2 + 1) * LANES]
        qim_ref[h] = jnp.where((lane // IDX_DIM) == (h % 2), qp, zero)

    def score_block(j):
        start = pl.multiple_of(j * tk, tk)
        kb = k2_ref[0, pl.ds(start, tk), :]
        sc = jnp.zeros((tk, tq), F32)
        for h in range(IDX_HEADS):
            lg = lax.dot_general(kb, qim_ref[h], _CONTRACT_LAST, preferred_element_type=F32)
            sc = sc + w_s[h:h + 1, :] * jnp.maximum(lg, 0.0)
        sc = jnp.where(sc == 0.0, 0.0, sc)
        bits = pltpu.bitcast(sc, jnp.int32)
        key = bits ^ ((bits >> 31) & 0x7FFFFFFF)
        key = jnp.where(start + krow <= qpos, key, KEY_NEG_INF)
        keyt_ref[j] = key
        hi_ref[j] = (key >> 16).astype(jnp.int16)
        lo_ref[j] = ((key & 0xFFFF) - HALF).astype(jnp.int16)

    def score_pair(t, carry):
        score_block(2 * t)
        score_block(2 * t + 1)
        return carry

    def score_last(j, carry):
        score_block(j)
        return carry

    lax.fori_loop(0, nsb // 2, score_pair, 0)
    lax.fori_loop((nsb // 2) * 2, nsb, score_last, 0)
    lowest = jnp.full((tk, tq), -HALF, jnp.int16)
    hi_ref[nsb] = lowest
    lo_ref[nsb] = lowest

    def tree_sum(x, rows):
        parts = [x[r * rows:(r + 1) * rows] for r in range(x.shape[0] // rows)]
        while len(parts) > 1:
            parts = [a + b for a, b in zip(parts[::2], parts[1::2])]
        return parts[0]

    def count(pred):
        def body(j, acc):
            return acc + tree_sum(jnp.where(pred(keyt_ref[j], j), 1.0, 0.0), sub)
        per_sublane = lax.fori_loop(0, nsb, body, jnp.zeros((sub, tq), F32))
        return jnp.sum(per_sublane, axis=0, keepdims=True)

    def count16(ref, pred):
        one, none = jnp.ones((), jnp.int16), jnp.zeros((), jnp.int16)

        def body(t, acc):
            x = jnp.where(pred(ref[2 * t]), one, none) + jnp.where(pred(ref[2 * t + 1]), one, none)
            return acc + tree_sum(x, 2 * sub)
        per_row = lax.fori_loop(0, (nsb + 1) // 2, body, jnp.zeros((2 * sub, tq), jnp.int16))
        return jnp.sum(per_row.astype(jnp.int32), axis=0, keepdims=True)

    def search16(ref, want):
        def step(t, u):
            cand_u = u | lax.shift_left(jnp.int32(1), 15 - t)
            cand = (cand_u - HALF).astype(jnp.int16)
            n = count16(ref, lambda v: v >= cand)
            return jnp.where(n >= want, cand_u, u)
        return lax.fori_loop(0, 16, step, jnp.zeros((1, tq), jnp.int32)) - HALF

    t_hi = search16(hi_ref, k_top)
    t_hi16 = t_hi.astype(jnp.int16)
    n_above = count16(hi_ref, lambda v: v > t_hi16)

    def keep_ties(j, carry):
        lo_ref[j] = jnp.where(hi_ref[j] == t_hi16, lo_ref[j], jnp.asarray(-HALF, jnp.int16))
        return carry

    lax.fori_loop(0, nsb, keep_ties, 0)
    t_lo = search16(lo_ref, k_top - n_above)
    thr = lax.shift_left(t_hi, 16) | (t_lo + HALF)
    n_gt = count(lambda kk, j: kk > thr)
    n_ge = count(lambda kk, j: kk >= thr)
    need = k_top - n_gt
    excess = (n_ge > k_top) & (thr > KEY_NEG_INF)
    s_total = k2_ref.shape[1]

    def tie_search():
        def step(t, p):
            cand = p | lax.shift_left(jnp.int32(1), (s_total.bit_length() - 1) - t)
            n = count(lambda kk, j: (kk == thr) & (j * tk + krow <= cand - 1))
            return jnp.where(n < need, cand, p)
        p = lax.fori_loop(0, s_total.bit_length(), step, jnp.zeros((1, tq), jnp.int32))
        return jnp.where(excess, p, s_total)

    any_excess = jnp.max(jnp.where(excess, 1.0, 0.0)) > 0.0
    tie_limit = lax.cond(any_excess, tie_search, lambda: jnp.full((1, tq), s_total, jnp.int32))

    def select_block(j, carry):
        key = keyt_ref[j]
        kpos = j * tk + krow
        sel = ((key > thr) | ((key == thr) & (kpos <= tie_limit))) & (kpos <= qpos)
        selb_ref[j] = jnp.where(sel, 0.0, NEG_BIG).T
        return carry

    lax.fori_loop(0, nsb, select_block, 0)

    @pl.when(i == 0)
    def _():
        cf = c_ref[0].astype(F32)
        cmax_ref[...] = jnp.full(cmax_ref.shape, jnp.sqrt(jnp.max(jnp.sum(cf * cf, axis=-1, keepdims=True))))

    c_norm = cmax_ref[0:1, 0:1]
    tab = tab_ref[...]
    bias_max = (jnp.max(tab, axis=0, keepdims=True) - tab[T5_BUCKETS - 1:T5_BUCKETS, :]) * LOG2E
    for h in range(DSA_HEADS):
        rows = slice(h * tq, (h + 1) * tq)
        qp = q_ref[0, :, (h // 2) * LANES:(h // 2 + 1) * LANES]
        ql = jnp.dot(qp, wuk_ref[h], preferred_element_type=F32) * ((DSA_DH ** -0.5) * LOG2E)
        ql = ql.astype(BF16)
        qall_ref[rows, :] = ql
        qf = ql.astype(F32)
        bound = jnp.sqrt(jnp.sum(qf * qf, axis=-1, keepdims=True)) * c_norm + bias_max[:, h:h + 1]
        mx_ref[rows] = jnp.broadcast_to(bound, (tq, tk))
    l_ref[...] = jnp.zeros_like(l_ref)
    acc_ref[...] = jnp.zeros_like(acc_ref)
    n_far = jnp.maximum(((i - N_BIAS_TILES) * tq) // tk + 1, 0)

    def logits(j0, nblk, near):
        rc = tq // 2
        start = pl.multiple_of(j0 * tk, tk)
        cb = c_ref[0, pl.ds(start, nblk * tk), :]
        s_all = lax.dot_general(qall_ref[...], cb, _CONTRACT_LAST, preferred_element_type=F32)
        for h in range(DSA_HEADS):
            for r0 in range(0, tq, rc):
                rows = slice(h * tq + r0, h * tq + r0 + rc)
                tiles = []
                for b in range(nblk):
                    s = s_all[rows, b * tk:(b + 1) * tk] + selb_ref[j0 + b, r0:r0 + rc, :]
                    if near:
                        s = s + bias_ref[(i * tq - (j0 + b) * tk) // tq, h, r0:r0 + rc, :]
                    tiles.append(s)
                yield rows, tiles, cb

    def max_blocks(j0, nblk, near):
        for rows, tiles, _ in logits(j0, nblk, near):
            m = mx_ref[rows]
            for s in tiles:
                m = jnp.maximum(m, s)
            mx_ref[rows] = m

    def sum_blocks(j0, nblk, near):
        for rows, tiles, cb in logits(j0, nblk, near):
            m = mx_ref[rows]
            l = l_ref[rows]
            for b, s in enumerate(tiles):
                p = jnp.exp2(s - m)
                l = l + p
                p_ref[rows, b * tk:(b + 1) * tk] = p.astype(BF16)
            l_ref[rows] = l
        acc_ref[...] += jnp.dot(p_ref[:, :nblk * tk], cb, preferred_element_type=F32)

    def sweep(fn):
        def loop(lo, hi, stride, near):
            def body(t, carry):
                fn(lo + t * stride, stride, near)
                return carry
            lax.fori_loop(0, (hi - lo) // stride, body, 0)
        lo, stride = 0, FAR_STEP
        while stride >= 1:
            hi = lo + ((n_far - lo) // stride) * stride
            loop(lo, hi, stride, False)
            lo, stride = hi, stride // 2
        loop(n_far, nsb, 1, True)

    def row_sums():
        smallest = None
        for h in range(DSA_HEADS):
            rows = slice(h * tq, (h + 1) * tq)
            ls = jnp.sum(l_ref[rows], axis=-1, keepdims=True)
            lsum_ref[rows] = ls
            smallest = ls if smallest is None else jnp.minimum(smallest, ls)
        return jnp.min(smallest)

    sweep(sum_blocks)

    @pl.when(row_sums() < UNDERFLOW_GUARD)
    def _():
        mx_ref[...] = jnp.full_like(mx_ref, NEG_BIG)
        l_ref[...] = jnp.zeros_like(l_ref)
        acc_ref[...] = jnp.zeros_like(acc_ref)
        sweep(max_blocks)
        for h in range(DSA_HEADS):
            rows = slice(h * tq, (h + 1) * tq)
            mx_ref[rows] = jnp.broadcast_to(jnp.max(mx_ref[rows], axis=-1, keepdims=True), (tq, tk))
        sweep(sum_blocks)
        row_sums()

    for pair in range(DSA_HEADS // 2):
        out = jnp.zeros((tq, LANES), F32)
        for h in (2 * pair, 2 * pair + 1):
            rows = slice(h * tq, (h + 1) * tq)
            o_lat = (acc_ref[rows] / lsum_ref[rows]).astype(BF16)
            out = out + jnp.dot(o_lat, wuv_ref[h], preferred_element_type=F32)
        o_ref[0, :, pair * LANES:(pair + 1) * LANES] = out.astype(BF16)


def _dsa(big, w_idx_t, c, w_uk, w_uv, bias_tiles, t5_table, q_col, qi_col, k2_col):
    b, s, _ = big.shape
    k_top = min(TOPK_MAX, s // 4)
    hq = DSA_HEADS * DSA_DH
    hi = IDX_HEADS * IDX_DIM
    wuk = jnp.zeros((DSA_HEADS, LANES, DSA_LATENT), F32)
    wuv = jnp.zeros((DSA_HEADS, DSA_LATENT, LANES), F32)
    even = np.arange(0, DSA_HEADS, 2)
    wuk = wuk.at[even, :DSA_DH].set(w_uk[even]).at[even + 1, DSA_DH:].set(w_uk[even + 1]).astype(BF16)
    wuv = wuv.at[even, :, :DSA_DH].set(w_uv[even]).at[even + 1, :, DSA_DH:].set(w_uv[even + 1]).astype(BF16)
    nq = s // DSA_TQ
    rows = DSA_HEADS * DSA_TQ
    return pl.pallas_call(
        functools.partial(_dsa_kernel, k_top=k_top),
        grid=(b, nq),
        in_specs=[
            pl.BlockSpec((1, DSA_TQ, hq), lambda bi, qi: (bi, qi, q_col)),
            pl.BlockSpec((1, DSA_TQ, hi), lambda bi, qi: (bi, qi, qi_col)),
            pl.BlockSpec((1, IDX_HEADS, DSA_TQ), lambda bi, qi: (bi, 0, qi)),
            pl.BlockSpec((1, s, LANES), lambda bi, qi: (bi, 0, k2_col)),
            pl.BlockSpec((1, s, DSA_LATENT), lambda bi, qi: (bi, 0, 0)),
            _resident(wuk.shape), _resident(wuv.shape), _resident(bias_tiles.shape), _resident(t5_table.shape),
        ],
        out_specs=pl.BlockSpec((1, DSA_TQ, hq), lambda bi, qi: (bi, qi, 0)),
        out_shape=jax.ShapeDtypeStruct((b, s, hq), BF16),
        scratch_shapes=[
            pltpu.VMEM((s // DSA_TK, DSA_TK, DSA_TQ), jnp.int32),
            pltpu.VMEM((s // DSA_TK + 1, DSA_TK, DSA_TQ), jnp.int16),
            pltpu.VMEM((s // DSA_TK + 1, DSA_TK, DSA_TQ), jnp.int16),
            pltpu.VMEM((s // DSA_TK, DSA_TQ, DSA_TK), F32),
            pltpu.VMEM((IDX_HEADS, DSA_TQ, LANES), BF16),
            pltpu.VMEM((rows, DSA_LATENT), BF16),
            pltpu.VMEM((rows, FAR_STEP * DSA_TK), BF16),
            pltpu.VMEM((rows, DSA_TK), F32),
            pltpu.VMEM((rows, DSA_TK), F32),
            pltpu.VMEM((rows, DSA_LATENT), F32),
            pltpu.VMEM((rows, 1), F32),
            pltpu.VMEM((8, LANES), F32),
        ],
        compiler_params=_params("parallel", "arbitrary"),
        name="dsa",
    )(big, big, w_idx_t, big, c, wuk, wuv, bias_tiles, t5_table)


def _even_mixer(x2, b, s, g, w_in, w_gate, b_gate, norm_g, fox_b, w_out):
    hk = GLA_HEADS * GLA_DK
    hv = GLA_HEADS * GLA_DV
    hf = FOX_HEADS * FOX_DH
    o = np.cumsum([0, hk, hk, hv, hv, GLA_GATE_RANK, hf, hf, hf, FOX_HEADS])
    col = lambda n: w_in[:, o[n]:o[n + 1]]
    w_big = jnp.concatenate([col(0), col(1), col(2), col(3), col(5), col(6), col(7)], axis=1)
    w_small = jnp.zeros((w_in.shape[0], LANES), F32)
    w_small = w_small.at[:, :GLA_GATE_RANK].set(col(4)).at[:, GLA_GATE_RANK:GLA_GATE_RANK + FOX_HEADS].set(col(8))
    big, small = _inproj(x2, g, w_big, w_small)
    big = big.reshape(b, s, -1)
    small = small.reshape(b, s, LANES)
    o_gla = _gla(big, small, w_gate, b_gate, norm_g)
    f_logit_t = small[:, :, GLA_GATE_RANK:GLA_GATE_RANK + FOX_HEADS].transpose(0, 2, 1)
    f_cum = _fox_gate(f_logit_t, fox_b)
    fox_col = (2 * hk + 2 * hv) // hf
    o_fox = _fox(big, f_cum, fox_col, fox_col + 1, fox_col + 2)
    return _outproj(x2, [o_gla.reshape(b * s, hv), o_fox.reshape(b * s, hf)], w_out)


def _odd_mixer(x2, b, s, g, w_in, kv_g, w_uk, w_uv, w_out, bias_tiles, t5_table):
    hq = DSA_HEADS * DSA_DH
    hi = IDX_HEADS * IDX_DIM
    o = np.cumsum([0, hq, DSA_LATENT, hi, IDX_DIM, IDX_HEADS])
    col = lambda n: w_in[:, o[n]:o[n + 1]]
    w_big = jnp.concatenate([col(0), col(2), col(3), col(3)], axis=1)
    w_small = jnp.zeros((w_in.shape[0], LANES), F32).at[:, :IDX_HEADS].set(col(4))
    big, small, c = _inproj(x2, g, w_big, w_small, col(1), kv_g)
    big = big.reshape(b, s, -1)
    w_idx_t = small.reshape(b, s, LANES)[:, :, :IDX_HEADS].transpose(0, 2, 1)
    o_dsa = _dsa(big, w_idx_t, c.reshape(b, s, DSA_LATENT), w_uk, w_uv, bias_tiles, t5_table,
                 0, hq // hi, (hq + hi) // LANES)
    return _outproj(x2, [o_dsa.reshape(b * s, hq)], w_out)


def kernel(x, norm_g, ffn_w_in, ffn_w_out, even_w_in, gla_w_gate, gla_b_gate, gla_norm_g, fox_b_f, even_w_out,
           odd_w_in, mla_kv_norm_g, mla_w_uk, mla_w_uv, odd_w_out, t5_table, final_norm_g):
    b, s, d = x.shape
    depth = norm_g.shape[0]
    x2 = x.reshape(b * s, d)
    bias_tiles = _bias_tiles(t5_table) if depth > 1 else None
    for layer in range(depth):
        g = norm_g[layer]
        j = layer // 2
        x2 = _ffn(x2, g[0], ffn_w_in[layer, 0], ffn_w_out[layer, 0])
        if layer % 2 == 0:
            x2 = _even_mixer(x2, b, s, g[1], even_w_in[j], gla_w_gate[j], gla_b_gate[j], gla_norm_g[j],
                             fox_b_f[j], even_w_out[j])
        else:
            x2 = _odd_mixer(x2, b, s, g[1], odd_w_in[j], mla_kv_norm_g[j], mla_w_uk[j], mla_w_uv[j],
                            odd_w_out[j], bias_tiles, t5_table)
        last = layer == depth - 1
        x2 = _ffn(x2, g[2], ffn_w_in[layer, 1], ffn_w_out[layer, 1], final_norm_g if last else None)
    return x2.reshape(b, s, d)
```

```python
import functools
import math

import numpy as np
import jax
import jax.numpy as jnp
from jax import lax
from jax.experimental import pallas as pl
from jax.experimental.pallas import tpu as pltpu

F32 = jnp.float32
BF16 = jnp.bfloat16

EPS = 1e-6
GLA_HEADS = 4
GLA_DK = 64
GLA_DV = 128
GLA_GATE_RANK = 16
GLA_GATE_NORMALIZER = 16.0
GLA_CHUNK = 64
FOX_HEADS = 8
FOX_DH = 64
DSA_HEADS = 16
DSA_DH = 64
DSA_LATENT = 256
IDX_HEADS = 8
IDX_DIM = 64
TOPK_MAX = 256
T5_BUCKETS = 32
T5_MAX_DIST = 128

LANES = 128
VMEM_LIMIT = 56 * 1024 * 1024

ROW_TILE = 512
FFN_ROW_TILE = 1024
FFN_CHUNK = 256
GLA_ROWS = 256
FOX_TQ = 512
DSA_TQ = 128
DSA_TK = 256
FAR_STEP = 4
NEG_BIG = -1e30
UNDERFLOW_GUARD = 2.0 ** -60
BOUND_SLACK = 1.01
INT_MIN = -2 ** 31
KEY_NEG_INF = INT_MIN + 0x7FFFFF
HALF = 1 << 15

_CONTRACT_LAST = (((1,), (1,)), ((), ()))
_CONTRACT_FIRST = (((0,), (0,)), ((), ()))


def _params(*sem):
    return pltpu.CompilerParams(dimension_semantics=sem, vmem_limit_bytes=VMEM_LIMIT)


def _resident(shape):
    nd = len(shape)
    return pl.BlockSpec(shape, lambda *_: (0,) * nd, pipeline_mode=pl.Buffered(1))


def _rms(x, g):
    return x * lax.rsqrt(jnp.mean(x * x, axis=-1, keepdims=True) + EPS) * g


def _log_sigmoid(x):
    return jnp.minimum(x, 0.0) - jnp.log1p(jnp.exp(-jnp.abs(x)))


def _silu(x):
    return x * jax.nn.sigmoid(x)


def _ffn_kernel(*refs, n_chunks, final, n_mix):
    x_ref, g_ref, wi_ref, wo_ref = refs[:4]
    mix_refs = refs[4:4 + n_mix]
    proj_refs = refs[4 + n_mix:4 + 2 * n_mix]
    rest = refs[4 + 2 * n_mix:]
    if final:
        gf_ref, o_ref, h_ref, acc_ref = rest
    else:
        o_ref, h_ref, acc_ref = rest
    x = x_ref[...]
    for m_ref, p_ref in zip(mix_refs, proj_refs):
        x = x + jnp.dot(m_ref[...], p_ref[...], preferred_element_type=F32)
    o_ref[...] = x
    h_ref[...] = _rms(x, g_ref[...]).astype(BF16)
    f = wo_ref.shape[0]
    for j in range(n_chunks):
        lo, hi = j * FFN_CHUNK, (j + 1) * FFN_CHUNK
        h = h_ref[...]
        a = jnp.dot(h, wi_ref[:, lo:hi], preferred_element_type=F32)
        b = jnp.dot(h, wi_ref[:, f + lo:f + hi], preferred_element_type=F32)
        act = (_silu(a) * b).astype(BF16)
        part = jnp.dot(act, wo_ref[lo:hi, :], preferred_element_type=F32)
        if j == 0:
            acc_ref[...] = part
        else:
            acc_ref[...] += part
    y = o_ref[...] + 0.5 * acc_ref[...]
    if final:
        y = _rms(y, gf_ref[...])
    o_ref[...] = y


def _ffn(x2, g, w_in, w_out, final_g=None, mixer=None):
    t, d = x2.shape
    f = w_out.shape[0]
    n_chunks = f // FFN_CHUNK
    wi = w_in.astype(BF16)
    wo = w_out.astype(BF16)
    final = final_g is not None
    row = lambda n: pl.BlockSpec((FFN_ROW_TILE, n), lambda i: (i, 0))
    in_specs = [row(d), _resident((1, d)), _resident(wi.shape), _resident(wo.shape)]
    args = [x2, g.reshape(1, d), wi, wo]
    outs, w_proj = mixer if mixer is not None else ((), None)
    lo = 0
    projs = []
    for o in outs:
        projs.append(w_proj[lo:lo + o.shape[1]].astype(BF16))
        lo += o.shape[1]
    in_specs += [row(o.shape[1]) for o in outs] + [_resident(p.shape) for p in projs]
    args += list(outs) + projs
    if final:
        in_specs.append(_resident((1, d)))
        args.append(final_g.reshape(1, d))
    return pl.pallas_call(
        functools.partial(_ffn_kernel, n_chunks=n_chunks, final=final, n_mix=len(outs)),
        grid=(t // FFN_ROW_TILE,),
        in_specs=in_specs,
        out_specs=row(d),
        out_shape=jax.ShapeDtypeStruct((t, d), F32),
        scratch_shapes=[pltpu.VMEM((FFN_ROW_TILE, d), BF16), pltpu.VMEM((FFN_ROW_TILE, d), F32)],
        compiler_params=_params("parallel"),
        name="ffn",
    )(*args)


def _inproj_kernel(*refs, big_chunks, with_c):
    if with_c:
        x_ref, g_ref, wbig_ref, wsm_ref, wc_ref, gc_ref, big_ref, sm_ref, c_ref = refs
    else:
        x_ref, g_ref, wbig_ref, wsm_ref, big_ref, sm_ref = refs
    h = _rms(x_ref[...], g_ref[...]).astype(BF16)
    for lo, hi in big_chunks:
        big_ref[:, lo:hi] = jnp.dot(h, wbig_ref[:, lo:hi], preferred_element_type=F32).astype(BF16)
    sm_ref[...] = jnp.dot(h, wsm_ref[...], preferred_element_type=F32)
    if with_c:
        ckv = jnp.dot(h, wc_ref[...], preferred_element_type=F32)
        c_ref[...] = _rms(ckv, gc_ref[...]).astype(BF16)


def _inproj(x2, g, w_big, w_small, w_c=None, g_c=None):
    t, d = x2.shape
    nbig = w_big.shape[1]
    with_c = w_c is not None
    step = 4 * LANES
    big_chunks = tuple((lo, min(lo + step, nbig)) for lo in range(0, nbig, step))
    row = lambda n: pl.BlockSpec((ROW_TILE, n), lambda i: (i, 0))
    in_specs = [row(d), _resident((1, d)), _resident(w_big.shape), _resident(w_small.shape)]
    args = [x2, g.reshape(1, d), w_big.astype(BF16), w_small.astype(BF16)]
    out_specs = [row(nbig), row(LANES)]
    out_shape = [jax.ShapeDtypeStruct((t, nbig), BF16), jax.ShapeDtypeStruct((t, LANES), F32)]
    if with_c:
        nc = w_c.shape[1]
        in_specs += [_resident(w_c.shape), _resident((1, nc))]
        args += [w_c.astype(BF16), g_c.reshape(1, nc)]
        out_specs.append(row(nc))
        out_shape.append(jax.ShapeDtypeStruct((t, nc), BF16))
    return pl.pallas_call(
        functools.partial(_inproj_kernel, big_chunks=big_chunks, with_c=with_c),
        grid=(t // ROW_TILE,),
        in_specs=in_specs,
        out_specs=out_specs,
        out_shape=out_shape,
        compiler_params=_params("parallel"),
        name="inproj_c" if with_c else "inproj",
    )(*args)


def _gla_kernel(q_ref, k_ref, v_ref, go_ref, sm_ref, wg_ref, bg_ref, ng_ref, o_ref, st_ref):
    rs = q_ref.shape[1]
    c_len = GLA_CHUNK

    @pl.when(pl.program_id(1) == 0)
    def _():
        st_ref[...] = jnp.zeros_like(st_ref)

    pre = jnp.dot(sm_ref[0].astype(BF16), wg_ref[...], preferred_element_type=F32) + bg_ref[...]
    gk = _log_sigmoid(pre) / GLA_GATE_NORMALIZER
    r = lax.broadcasted_iota(jnp.int32, (rs, rs), 0)
    c = lax.broadcasted_iota(jnp.int32, (rs, rs), 1)
    same = (r // c_len) == (c // c_len)
    tri = jnp.where(same & (c <= r), 1.0, 0.0).astype(F32)
    ones = jnp.where(same, 1.0, 0.0).astype(F32)
    g_cum = jnp.dot(tri, gk, precision=lax.Precision.HIGHEST, preferred_element_type=F32)
    g_last = jnp.dot(ones, gk, precision=lax.Precision.HIGHEST, preferred_element_type=F32)
    q = q_ref[0].astype(F32) * (GLA_DK ** -0.5)
    k = k_ref[0].astype(F32)
    q_dec = (q * jnp.exp(g_cum)).astype(BF16)
    k_dec = (k * jnp.exp(-g_cum)).astype(BF16)
    k_end = (k * jnp.exp(g_last - g_cum)).astype(BF16)
    decay = jnp.exp(g_last)

    lane = lax.broadcasted_iota(jnp.int32, (1, LANES), 1)
    ri = lax.broadcasted_iota(jnp.int32, (c_len, c_len), 0)
    ci = lax.broadcasted_iota(jnp.int32, (c_len, c_len), 1)
    causal = ci <= ri
    ng = ng_ref[...]
    zero = jnp.zeros((), BF16)
    for ch in range(rs // c_len):
        rows = slice(ch * c_len, (ch + 1) * c_len)
        for h in range(GLA_HEADS):
            pair = slice((h // 2) * LANES, (h // 2 + 1) * LANES)
            mine = (lane // GLA_DK) == (h % 2)
            vcol = slice(h * GLA_DV, (h + 1) * GLA_DV)
            qm = jnp.where(mine, q_dec[rows, pair], zero)
            a = lax.dot_general(qm, k_dec[rows, pair], _CONTRACT_LAST, preferred_element_type=F32)
            a = jnp.where(causal, a, 0.0)
            vh = v_ref[0, rows, vcol]
            st = st_ref[h]
            o = jnp.dot(a.astype(BF16), vh, preferred_element_type=F32)
            o = o + lax.dot_general(qm, st.astype(BF16), _CONTRACT_LAST, preferred_element_type=F32)
            km = jnp.where(mine, k_end[rows, pair], zero)
            upd = lax.dot_general(vh, km, _CONTRACT_FIRST, preferred_element_type=F32)
            st_ref[h] = st * decay[ch * c_len:ch * c_len + 1, pair] + upd
            on = _rms(o, ng)
            gate = go_ref[0, rows, vcol].astype(F32)
            o_ref[0, rows, vcol] = (on * _silu(gate)).astype(BF16)


def _gla(big, small, w_gate, b_gate, norm_g):
    b, s, _ = big.shape
    hk = GLA_HEADS * GLA_DK
    hv = GLA_HEADS * GLA_DV
    wg = jnp.pad(w_gate, ((0, LANES - GLA_GATE_RANK), (0, 0))).astype(BF16)
    blk = lambda n, cb: pl.BlockSpec((1, GLA_ROWS, n), lambda bi, si: (bi, si, cb))
    return pl.pallas_call(
        _gla_kernel,
        grid=(b, s // GLA_ROWS),
        in_specs=[blk(hk, 0), blk(hk, 1), blk(hv, 1), blk(hv, 2), blk(LANES, 0),
                  _resident(wg.shape), _resident((1, hk)), _resident((1, GLA_DV))],
        out_specs=blk(hv, 0),
        out_shape=jax.ShapeDtypeStruct((b, s, hv), BF16),
        scratch_shapes=[pltpu.VMEM((GLA_HEADS, GLA_DV, LANES), F32)],
        compiler_params=_params("parallel", "arbitrary"),
        name="gla",
    )(big, big, big, big, small, wg, b_gate.reshape(1, hk), norm_g.reshape(1, GLA_DV))


def _foxgate_kernel(fl_ref, b_ref, f_ref):
    x = _log_sigmoid(fl_ref[0] + b_ref[...])
    s = x.shape[1]
    lane = lax.broadcasted_iota(jnp.int32, x.shape, 1)
    sh = 1
    while sh < s:
        x = x + jnp.where(lane >= sh, pltpu.roll(x, sh, 1), 0.0)
        sh *= 2
    f_ref[0] = x


def _fox_gate(f_logit_t, bias):
    b, h, s = f_logit_t.shape
    blk = pl.BlockSpec((1, h, s), lambda bi: (bi, 0, 0))
    return pl.pallas_call(
        _foxgate_kernel,
        grid=(b,),
        in_specs=[blk, _resident((h, 1))],
        out_specs=blk,
        out_shape=jax.ShapeDtypeStruct((b, h, s), F32),
        compiler_params=_params("parallel"),
        name="fox_gate",
    )(f_logit_t, bias.reshape(h, 1))


def _fox_kernel(q_ref, k_ref, v_ref, f_ref, fcol_ref, o_ref, qs_ref, mx_ref, l_ref, acc_ref, lsum_ref, kmax_ref):
    tq = q_ref.shape[1]
    tk = tq
    qi = pl.program_id(1)
    n_pairs = FOX_HEADS // 2
    lane = lax.broadcasted_iota(jnp.int32, (1, LANES), 1)
    zero = jnp.zeros((), BF16)

    @pl.when(qi == 0)
    def _():
        for pair in range(n_pairs):
            kf = k_ref[0, :, pair * LANES:(pair + 1) * LANES].astype(F32)
            kmax_ref[pair] = jnp.full(kmax_ref.shape[1:], jnp.sqrt(jnp.max(jnp.sum(kf * kf, axis=-1, keepdims=True))))

    f_rows = fcol_ref[0]
    for h in range(FOX_HEADS):
        rows = slice(h * tq, (h + 1) * tq)
        q = q_ref[0, :, (h // 2) * LANES:(h // 2 + 1) * LANES] * jnp.asarray(FOX_DH ** -0.5, BF16)
        q = jnp.where((lane // FOX_DH) == (h % 2), q, zero)
        qs_ref[rows, :] = q
        qf = q.astype(F32)
        n2 = jnp.dot((qf * qf).astype(BF16), jnp.ones((LANES, LANES), BF16), preferred_element_type=F32)
        bound = jnp.sqrt(n2) * (BOUND_SLACK * kmax_ref[h // 2, 0:1, 0:1]) - f_rows[:, h:h + 1]
        mx_ref[rows] = jnp.concatenate([bound] * (tk // LANES), axis=1)
    l_ref[...] = jnp.zeros_like(l_ref)
    acc_ref[...] = jnp.zeros_like(acc_ref)
    row = lax.broadcasted_iota(jnp.int32, (tq, tk), 0)
    col = lax.broadcasted_iota(jnp.int32, (tq, tk), 1)

    def logits(j, diagonal):
        start = pl.multiple_of(j * tk, tk)
        for pair in range(n_pairs):
            kb = k_ref[0, pl.ds(start, tk), pair * LANES:(pair + 1) * LANES]
            s_all = lax.dot_general(qs_ref[2 * pair * tq:2 * (pair + 1) * tq, :], kb, _CONTRACT_LAST,
                                    preferred_element_type=F32)
            for hh in range(2):
                h = 2 * pair + hh
                s = s_all[hh * tq:(hh + 1) * tq] - f_ref[0, h, pl.ds(j, 1), :]
                if diagonal:
                    s = jnp.where(col <= row, s, NEG_BIG)
                yield slice(h * tq, (h + 1) * tq), s, pair

    def max_block(j, diagonal):
        for rows, s, _ in logits(j, diagonal):
            mx_ref[rows] = jnp.maximum(mx_ref[rows], s)

    def sum_block(j, diagonal):
        start = pl.multiple_of(j * tk, tk)
        for rows, s, pair in logits(j, diagonal):
            p = jnp.exp(s - mx_ref[rows])
            l_ref[rows] += p
            vb = v_ref[0, pl.ds(start, tk), pair * LANES:(pair + 1) * LANES]
            acc_ref[rows] += jnp.dot(p.astype(BF16), vb, preferred_element_type=F32)

    def loop(fn):
        def body(j, carry):
            fn(j, False)
            return carry
        lax.fori_loop(0, qi, body, 0)
        fn(qi, True)

    def row_sums():
        smallest = None
        for h in range(FOX_HEADS):
            rows = slice(h * tq, (h + 1) * tq)
            ls = jnp.sum(l_ref[rows], axis=-1, keepdims=True)
            lsum_ref[rows] = ls
            smallest = ls if smallest is None else jnp.minimum(smallest, ls)
        return jnp.min(smallest)

    loop(sum_block)

    @pl.when(row_sums() < UNDERFLOW_GUARD)
    def _():
        mx_ref[...] = jnp.full_like(mx_ref, NEG_BIG)
        l_ref[...] = jnp.zeros_like(l_ref)
        acc_ref[...] = jnp.zeros_like(acc_ref)
        loop(max_block)
        for h in range(FOX_HEADS):
            rows = slice(h * tq, (h + 1) * tq)
            mx_ref[rows] = jnp.broadcast_to(jnp.max(mx_ref[rows], axis=-1, keepdims=True), (tq, tk))
        loop(sum_block)
        row_sums()

    for pair in range(n_pairs):
        o = [acc_ref[h * tq:(h + 1) * tq] / lsum_ref[h * tq:(h + 1) * tq] for h in (2 * pair, 2 * pair + 1)]
        o_ref[0, :, pair * LANES:(pair + 1) * LANES] = jnp.where((lane // FOX_DH) == 0, o[0], o[1]).astype(BF16)


def _fox(big, f_cum, q_col, k_col, v_col):
    b, s, _ = big.shape
    hf = FOX_HEADS * FOX_DH
    nkb = s // FOX_TQ
    f4 = f_cum.reshape(b, FOX_HEADS, nkb, FOX_TQ)
    rows = FOX_HEADS * FOX_TQ
    return pl.pallas_call(
        _fox_kernel,
        grid=(b, s // FOX_TQ),
        in_specs=[
            pl.BlockSpec((1, FOX_TQ, hf), lambda bi, qi: (bi, qi, q_col)),
            pl.BlockSpec((1, s, hf), lambda bi, qi: (bi, 0, k_col)),
            pl.BlockSpec((1, s, hf), lambda bi, qi: (bi, 0, v_col)),
            pl.BlockSpec((1, FOX_HEADS, nkb, FOX_TQ), lambda bi, qi: (bi, 0, 0, 0)),
            pl.BlockSpec((1, FOX_TQ, FOX_HEADS), lambda bi, qi: (bi, qi, 0)),
        ],
        out_specs=pl.BlockSpec((1, FOX_TQ, hf), lambda bi, qi: (bi, qi, 0)),
        out_shape=jax.ShapeDtypeStruct((b, s, hf), BF16),
        scratch_shapes=[pltpu.VMEM((rows, LANES), BF16),
                        pltpu.VMEM((rows, FOX_TQ), F32),
                        pltpu.VMEM((rows, FOX_TQ), F32),
                        pltpu.VMEM((rows, LANES), F32),
                        pltpu.VMEM((rows, 1), F32),
                        pltpu.VMEM((FOX_HEADS // 2, 8, LANES), F32)],
        compiler_params=_params("parallel", "arbitrary"),
        name="fox",
    )(big, big, big, f4, f_cum.transpose(0, 2, 1))


def _t5_bucket_np(dist):
    max_exact = T5_BUCKETS // 2
    d = np.maximum(dist, 1).astype(np.float32)
    large = max_exact + (np.log(d / np.float32(max_exact)) / np.float32(math.log(T5_MAX_DIST / max_exact))
                         * np.float32(T5_BUCKETS - max_exact)).astype(np.int32)
    large = np.minimum(large, T5_BUCKETS - 1)
    return np.where(dist < max_exact, dist, large).astype(np.int32)


N_BIAS_TILES = -(-(DSA_TK + T5_MAX_DIST - 1) // DSA_TQ)
LOG2E = math.log2(math.e)


def _bias_bucket_tiles():
    r = np.arange(DSA_TQ)[:, None]
    c = np.arange(DSA_TK)[None, :]
    return np.stack([_t5_bucket_np(np.maximum(t * DSA_TQ + r - c, 0)) for t in range(N_BIAS_TILES)])


def _bias_kernel(tab_ref, bkt_ref, o_ref):
    h = pl.program_id(1)
    bkt = bkt_ref[0]
    acc = jnp.zeros(bkt.shape, F32)
    for b in range(T5_BUCKETS):
        acc = jnp.where(bkt == b, tab_ref[b, h], acc)
    o_ref[0, 0] = (acc - tab_ref[T5_BUCKETS - 1, h]) * LOG2E


def _bias_tiles(t5_table):
    bkt = jnp.asarray(_bias_bucket_tiles())
    return pl.pallas_call(
        _bias_kernel,
        grid=(N_BIAS_TILES, DSA_HEADS),
        in_specs=[pl.BlockSpec(memory_space=pltpu.SMEM),
                  pl.BlockSpec((1, DSA_TQ, DSA_TK), lambda t, h: (t, 0, 0))],
        out_specs=pl.BlockSpec((1, 1, DSA_TQ, DSA_TK), lambda t, h: (t, h, 0, 0)),
        out_shape=jax.ShapeDtypeStruct((N_BIAS_TILES, DSA_HEADS, DSA_TQ, DSA_TK), F32),
        compiler_params=_params("parallel", "parallel"),
        name="t5_bias",
    )(t5_table, bkt)


def _dsa_kernel(q_ref, qi_ref, wt_ref, k2_ref, c_ref, wuk_ref, wuv_ref, bias_ref, tab_ref, o_ref,
                keyt_ref, hi_ref, lo_ref, selb_ref, qim_ref, qall_ref, p_ref, mx_ref, l_ref, acc_ref,
                lsum_ref, cmax_ref, *, k_top):
    tq, tk = DSA_TQ, DSA_TK
    sub = 8
    i = pl.program_id(1)
    nsb = (i * tq + tq + tk - 1) // tk
    lane = lax.broadcasted_iota(jnp.int32, (1, LANES), 1)
    qpos = i * tq + lax.broadcasted_iota(jnp.int32, (1, tq), 1)
    krow = lax.broadcasted_iota(jnp.int32, (tk, 1), 0)
    zero = jnp.zeros((), BF16)

    w_s = wt_ref[0] * ((IDX_HEADS ** -0.5) * (IDX_DIM ** -0.5))
    for h in range(IDX_HEADS):
        qp = qi_ref[0, :, (h // 2) * LANES:(h // 2 + 1) * LANES]
        qim_ref[h] = jnp.where((lane // IDX_DIM) == (h % 2), qp, zero)

    def score_block(j):
        start = pl.multiple_of(j * tk, tk)
        kb = k2_ref[0, pl.ds(start, tk), :]
        sc = jnp.zeros((tk, tq), F32)
        for h in range(IDX_HEADS):
            lg = lax.dot_general(kb, qim_ref[h], _CONTRACT_LAST, preferred_element_type=F32)
            sc = sc + w_s[h:h + 1, :] * jnp.maximum(lg, 0.0)
        sc = jnp.where(sc == 0.0, 0.0, sc)
        bits = pltpu.bitcast(sc, jnp.int32)
        key = bits ^ ((bits >> 31) & 0x7FFFFFFF)
        key = jnp.where(start + krow <= qpos, key, KEY_NEG_INF)
        keyt_ref[j] = key
        hi_ref[j] = (key >> 16).astype(jnp.int16)
        lo_ref[j] = ((key & 0xFFFF) - HALF).astype(jnp.int16)

    def score_pair(t, carry):
        score_block(2 * t)
        score_block(2 * t + 1)
        return carry

    def score_last(j, carry):
        score_block(j)
        return carry

    lax.fori_loop(0, nsb // 2, score_pair, 0)
    lax.fori_loop((nsb // 2) * 2, nsb, score_last, 0)
    lowest = jnp.full((tk, tq), -HALF, jnp.int16)
    hi_ref[nsb] = lowest
    lo_ref[nsb] = lowest

    def tree_sum(x, rows):
        parts = [x[r * rows:(r + 1) * rows] for r in range(x.shape[0] // rows)]
        while len(parts) > 1:
            parts = [a + b for a, b in zip(parts[::2], parts[1::2])]
        return parts[0]

    def count(pred):
        def body(j, acc):
            return acc + tree_sum(jnp.where(pred(keyt_ref[j], j), 1.0, 0.0), sub)
        per_sublane = lax.fori_loop(0, nsb, body, jnp.zeros((sub, tq), F32))
        return jnp.sum(per_sublane, axis=0, keepdims=True)

    def count16(ref, pred):
        one, none = jnp.ones((), jnp.int16), jnp.zeros((), jnp.int16)

        def body(t, acc):
            x = jnp.where(pred(ref[2 * t]), one, none) + jnp.where(pred(ref[2 * t + 1]), one, none)
            return acc + tree_sum(x, 2 * sub)
        per_row = lax.fori_loop(0, (nsb + 1) // 2, body, jnp.zeros((2 * sub, tq), jnp.int16))
        return jnp.sum(per_row.astype(jnp.int32), axis=0, keepdims=True)

    def search16(ref, want):
        def step(t, u):
            cand_u = u | lax.shift_left(jnp.int32(1), 15 - t)
            cand = (cand_u - HALF).astype(jnp.int16)
            n = count16(ref, lambda v: v >= cand)
            return jnp.where(n >= want, cand_u, u)
        return lax.fori_loop(0, 16, step, jnp.zeros((1, tq), jnp.int32)) - HALF

    t_hi = search16(hi_ref, k_top)
    t_hi16 = t_hi.astype(jnp.int16)
    n_above = count16(hi_ref, lambda v: v > t_hi16)

    def keep_ties(j, carry):
        lo_ref[j] = jnp.where(hi_ref[j] == t_hi16, lo_ref[j], jnp.asarray(-HALF, jnp.int16))
        return carry

    lax.fori_loop(0, nsb, keep_ties, 0)
    t_lo = search16(lo_ref, k_top - n_above)
    thr = lax.shift_left(t_hi, 16) | (t_lo + HALF)
    n_gt = count(lambda kk, j: kk > thr)
    n_ge = count(lambda kk, j: kk >= thr)
    need = k_top - n_gt
    excess = (n_ge > k_top) & (thr > KEY_NEG_INF)
    s_total = k2_ref.shape[1]

    def tie_search():
        def step(t, p):
            cand = p | lax.shift_left(jnp.int32(1), (s_total.bit_length() - 1) - t)
            n = count(lambda kk, j: (kk == thr) & (j * tk + krow <= cand - 1))
            return jnp.where(n < need, cand, p)
        p = lax.fori_loop(0, s_total.bit_length(), step, jnp.zeros((1, tq), jnp.int32))
        return jnp.where(excess, p, s_total)

    any_excess = jnp.max(jnp.where(excess, 1.0, 0.0)) > 0.0
    tie_limit = lax.cond(any_excess, tie_search, lambda: jnp.full((1, tq), s_total, jnp.int32))

    def select_block(j, carry):
        key = keyt_ref[j]
        kpos = j * tk + krow
        sel = ((key > thr) | ((key == thr) & (kpos <= tie_limit))) & (kpos <= qpos)
        selb_ref[j] = jnp.where(sel, 0.0, NEG_BIG).T
        return carry

    lax.fori_loop(0, nsb, select_block, 0)

    @pl.when(i == 0)
    def _():
        cf = c_ref[0].astype(F32)
        cmax_ref[...] = jnp.full(cmax_ref.shape, jnp.sqrt(jnp.max(jnp.sum(cf * cf, axis=-1, keepdims=True))))

    c_norm = cmax_ref[0:1, 0:1]
    tab = tab_ref[...]
    bias_max = (jnp.max(tab, axis=0, keepdims=True) - tab[T5_BUCKETS - 1:T5_BUCKETS, :]) * LOG2E
    for h in range(DSA_HEADS):
        rows = slice(h * tq, (h + 1) * tq)
        qp = q_ref[0, :, (h // 2) * LANES:(h // 2 + 1) * LANES]
        ql = jnp.dot(qp, wuk_ref[h], preferred_element_type=F32) * ((DSA_DH ** -0.5) * LOG2E)
        ql = ql.astype(BF16)
        qall_ref[rows, :] = ql
        qf = ql.astype(F32)
        n2 = jnp.dot((qf * qf).astype(BF16), jnp.ones((DSA_LATENT, LANES), BF16), preferred_element_type=F32)
        bound = jnp.sqrt(n2) * (BOUND_SLACK * c_norm) + bias_max[:, h:h + 1]
        mx_ref[rows] = jnp.concatenate([bound] * (tk // LANES), axis=1)
    l_ref[...] = jnp.zeros_like(l_ref)
    acc_ref[...] = jnp.zeros_like(acc_ref)
    n_far = jnp.maximum(((i - N_BIAS_TILES) * tq) // tk + 1, 0)

    def logits(j0, nblk, near):
        rc = tq // 2
        start = pl.multiple_of(j0 * tk, tk)
        cb = c_ref[0, pl.ds(start, nblk * tk), :]
        s_all = lax.dot_general(qall_ref[...], cb, _CONTRACT_LAST, preferred_element_type=F32)
        for h in range(DSA_HEADS):
            for r0 in range(0, tq, rc):
                rows = slice(h * tq + r0, h * tq + r0 + rc)
                tiles = []
                for b in range(nblk):
                    s = s_all[rows, b * tk:(b + 1) * tk] + selb_ref[j0 + b, r0:r0 + rc, :]
                    if near:
                        s = s + bias_ref[(i * tq - (j0 + b) * tk) // tq, h, r0:r0 + rc, :]
                    tiles.append(s)
                yield rows, tiles, cb

    def max_blocks(j0, nblk, near):
        for rows, tiles, _ in logits(j0, nblk, near):
            m = mx_ref[rows]
            for s in tiles:
                m = jnp.maximum(m, s)
            mx_ref[rows] = m

    def sum_blocks(j0, nblk, near):
        for rows, tiles, cb in logits(j0, nblk, near):
            m = mx_ref[rows]
            l = l_ref[rows]
            for b, s in enumerate(tiles):
                p = jnp.exp2(s - m)
                l = l + p
                p_ref[rows, b * tk:(b + 1) * tk] = p.astype(BF16)
            l_ref[rows] = l
        acc_ref[...] += jnp.dot(p_ref[:, :nblk * tk], cb, preferred_element_type=F32)

    def sweep(fn):
        def loop(lo, hi, stride, near):
            def body(t, carry):
                fn(lo + t * stride, stride, near)
                return carry
            lax.fori_loop(0, (hi - lo) // stride, body, 0)
        lo, stride = 0, FAR_STEP
        while stride >= 1:
            hi = lo + ((n_far - lo) // stride) * stride
            loop(lo, hi, stride, False)
            lo, stride = hi, stride // 2
        loop(n_far, nsb, 1, True)

    def row_sums():
        smallest = None
        for h in range(DSA_HEADS):
            rows = slice(h * tq, (h + 1) * tq)
            ls = jnp.sum(l_ref[rows], axis=-1, keepdims=True)
            lsum_ref[rows] = ls
            smallest = ls if smallest is None else jnp.minimum(smallest, ls)
        return jnp.min(smallest)

    sweep(sum_blocks)

    @pl.when(row_sums() < UNDERFLOW_GUARD)
    def _():
        mx_ref[...] = jnp.full_like(mx_ref, NEG_BIG)
        l_ref[...] = jnp.zeros_like(l_ref)
        acc_ref[...] = jnp.zeros_like(acc_ref)
        sweep(max_blocks)
        for h in range(DSA_HEADS):
            rows = slice(h * tq, (h + 1) * tq)
            mx_ref[rows] = jnp.broadcast_to(jnp.max(mx_ref[rows], axis=-1, keepdims=True), (tq, tk))
        sweep(sum_blocks)
        row_sums()

    for pair in range(DSA_HEADS // 2):
        out = jnp.zeros((tq, LANES), F32)
        for h in (2 * pair, 2 * pair + 1):
            rows = slice(h * tq, (h + 1) * tq)
            o_lat = (acc_ref[rows] / lsum_ref[rows]).astype(BF16)
            out = out + jnp.dot(o_lat, wuv_ref[h], preferred_element_type=F32)
        o_ref[0, :, pair * LANES:(pair + 1) * LANES] = out.astype(BF16)


def _dsa(big, w_idx_t, c, w_uk, w_uv, bias_tiles, t5_table, q_col, qi_col, k2_col):
    b, s, _ = big.shape
    k_top = min(TOPK_MAX, s // 4)
    hq = DSA_HEADS * DSA_DH
    hi = IDX_HEADS * IDX_DIM
    mine = (np.arange(LANES)[None, :] // DSA_DH) == (np.arange(DSA_HEADS)[:, None] % 2)
    wuk = jnp.where(mine[:, :, None], jnp.concatenate([w_uk, w_uk], axis=1), 0.0).astype(BF16)
    wuv = jnp.where(mine[:, None, :], jnp.concatenate([w_uv, w_uv], axis=2), 0.0).astype(BF16)
    nq = s // DSA_TQ
    rows = DSA_HEADS * DSA_TQ
    return pl.pallas_call(
        functools.partial(_dsa_kernel, k_top=k_top),
        grid=(b, nq),
        in_specs=[
            pl.BlockSpec((1, DSA_TQ, hq), lambda bi, qi: (bi, qi, q_col)),
            pl.BlockSpec((1, DSA_TQ, hi), lambda bi, qi: (bi, qi, qi_col)),
            pl.BlockSpec((1, IDX_HEADS, DSA_TQ), lambda bi, qi: (bi, 0, qi)),
            pl.BlockSpec((1, s, LANES), lambda bi, qi: (bi, 0, k2_col)),
            pl.BlockSpec((1, s, DSA_LATENT), lambda bi, qi: (bi, 0, 0)),
            _resident(wuk.shape), _resident(wuv.shape), _resident(bias_tiles.shape), _resident(t5_table.shape),
        ],
        out_specs=pl.BlockSpec((1, DSA_TQ, hq), lambda bi, qi: (bi, qi, 0)),
        out_shape=jax.ShapeDtypeStruct((b, s, hq), BF16),
        scratch_shapes=[
            pltpu.VMEM((s // DSA_TK, DSA_TK, DSA_TQ), jnp.int32),
            pltpu.VMEM((s // DSA_TK + 1, DSA_TK, DSA_TQ), jnp.int16),
            pltpu.VMEM((s // DSA_TK + 1, DSA_TK, DSA_TQ), jnp.int16),
            pltpu.VMEM((s // DSA_TK, DSA_TQ, DSA_TK), F32),
            pltpu.VMEM((IDX_HEADS, DSA_TQ, LANES), BF16),
            pltpu.VMEM((rows, DSA_LATENT), BF16),
            pltpu.VMEM((rows, FAR_STEP * DSA_TK), BF16),
            pltpu.VMEM((rows, DSA_TK), F32),
            pltpu.VMEM((rows, DSA_TK), F32),
            pltpu.VMEM((rows, DSA_LATENT), F32),
            pltpu.VMEM((rows, 1), F32),
            pltpu.VMEM((8, LANES), F32),
        ],
        compiler_params=_params("parallel", "arbitrary"),
        name="dsa",
    )(big, big, w_idx_t, big, c, wuk, wuv, bias_tiles, t5_table)


def _even_mixer(x2, b, s, g, w_in, w_gate, b_gate, norm_g, fox_b):
    hk = GLA_HEADS * GLA_DK
    hv = GLA_HEADS * GLA_DV
    hf = FOX_HEADS * FOX_DH
    o = np.cumsum([0, hk, hk, hv, hv, GLA_GATE_RANK, hf, hf, hf, FOX_HEADS])
    col = lambda n: w_in[:, o[n]:o[n + 1]]
    w_big = jnp.concatenate([col(0), col(1), col(2), col(3), col(5), col(6), col(7)], axis=1)
    w_small = jnp.pad(jnp.concatenate([col(4), col(8)], axis=1), ((0, 0), (0, LANES - GLA_GATE_RANK - FOX_HEADS)))
    big, small = _inproj(x2, g, w_big, w_small)
    big = big.reshape(b, s, -1)
    small = small.reshape(b, s, LANES)
    o_gla = _gla(big, small, w_gate, b_gate, norm_g)
    f_logit_t = small[:, :, GLA_GATE_RANK:GLA_GATE_RANK + FOX_HEADS].transpose(0, 2, 1)
    f_cum = _fox_gate(f_logit_t, fox_b)
    fox_col = (2 * hk + 2 * hv) // hf
    o_fox = _fox(big, f_cum, fox_col, fox_col + 1, fox_col + 2)
    return [o_gla.reshape(b * s, hv), o_fox.reshape(b * s, hf)]


def _odd_mixer(x2, b, s, g, w_in, kv_g, w_uk, w_uv, bias_tiles, t5_table):
    hq = DSA_HEADS * DSA_DH
    hi = IDX_HEADS * IDX_DIM
    o = np.cumsum([0, hq, DSA_LATENT, hi, IDX_DIM, IDX_HEADS])
    col = lambda n: w_in[:, o[n]:o[n + 1]]
    w_big = jnp.concatenate([col(0), col(2), col(3), col(3)], axis=1)
    w_small = jnp.pad(col(4), ((0, 0), (0, LANES - IDX_HEADS)))
    big, small, c = _inproj(x2, g, w_big, w_small, col(1), kv_g)
    big = big.reshape(b, s, -1)
    w_idx_t = small.reshape(b, s, LANES)[:, :, :IDX_HEADS].transpose(0, 2, 1)
    o_dsa = _dsa(big, w_idx_t, c.reshape(b, s, DSA_LATENT), w_uk, w_uv, bias_tiles, t5_table,
                 0, hq // hi, (hq + hi) // LANES)
    return [o_dsa.reshape(b * s, hq)]


def kernel(x, norm_g, ffn_w_in, ffn_w_out, even_w_in, gla_w_gate, gla_b_gate, gla_norm_g, fox_b_f, even_w_out,
           odd_w_in, mla_kv_norm_g, mla_w_uk, mla_w_uv, odd_w_out, t5_table, final_norm_g):
    b, s, d = x.shape
    depth = norm_g.shape[0]
    x2 = x.reshape(b * s, d)
    bias_tiles = _bias_tiles(t5_table) if depth > 1 else None
    for layer in range(depth):
        g = norm_g[layer]
        j = layer // 2
        x2 = _ffn(x2, g[0], ffn_w_in[layer, 0], ffn_w_out[layer, 0])
        if layer % 2 == 0:
            heads = _even_mixer(x2, b, s, g[1], even_w_in[j], gla_w_gate[j], gla_b_gate[j], gla_norm_g[j], fox_b_f[j])
            w_proj = even_w_out[j]
        else:
            heads = _odd_mixer(x2, b, s, g[1], odd_w_in[j], mla_kv_norm_g[j], mla_w_uk[j], mla_w_uv[j],
                               bias_tiles, t5_table)
            w_proj = odd_w_out[j]
        last = layer == depth - 1
        x2 = _ffn(x2, g[2], ffn_w_in[layer, 1], ffn_w_out[layer, 1], final_norm_g if last else None,
                  mixer=(heads, w_proj))
    return x2.reshape(b, s, d)
```

```python
import functools
import math

import numpy as np
import jax
import jax.numpy as jnp
from jax import lax
from jax.experimental import pallas as pl
from jax.experimental.pallas import tpu as pltpu

F32 = jnp.float32
BF16 = jnp.bfloat16

EPS = 1e-6
GLA_HEADS = 4
GLA_DK = 64
GLA_DV = 128
GLA_GATE_RANK = 16
GLA_GATE_NORMALIZER = 16.0
GLA_CHUNK = 64
FOX_HEADS = 8
FOX_DH = 64
DSA_HEADS = 16
DSA_DH = 64
DSA_LATENT = 256
IDX_HEADS = 8
IDX_DIM = 64
TOPK_MAX = 256
T5_BUCKETS = 32
T5_MAX_DIST = 128

LANES = 128
VMEM_LIMIT = 56 * 1024 * 1024

ROW_TILE = 512
FFN_ROW_TILE = 1024
FFN_CHUNK = 256
GLA_ROWS = 256
FOX_TQ = 512
DSA_TQ = 128
DSA_TK = 256
FAR_STEP = 4
COUNT_STEP = 4
NEG_BIG = -1e30
UNDERFLOW_GUARD = 2.0 ** -60
BOUND_SLACK = 1.01
INT_MIN = -2 ** 31
KEY_NEG_INF = INT_MIN + 0x7FFFFF
HALF = 1 << 15

_CONTRACT_LAST = (((1,), (1,)), ((), ()))
_CONTRACT_FIRST = (((0,), (0,)), ((), ()))


def _params(*sem):
    return pltpu.CompilerParams(dimension_semantics=sem, vmem_limit_bytes=VMEM_LIMIT)


def _resident(shape):
    nd = len(shape)
    return pl.BlockSpec(shape, lambda *_: (0,) * nd, pipeline_mode=pl.Buffered(1))


def _rms(x, g):
    return x * lax.rsqrt(jnp.mean(x * x, axis=-1, keepdims=True) + EPS) * g


def _log_sigmoid(x):
    return jnp.minimum(x, 0.0) - jnp.log1p(jnp.exp(-jnp.abs(x)))


def _silu(x):
    return x * jax.nn.sigmoid(x)


def _ffn_kernel(*refs, n_chunks, final, n_mix):
    x_ref, g_ref, wi_ref, wo_ref = refs[:4]
    mix_refs = refs[4:4 + n_mix]
    proj_refs = refs[4 + n_mix:4 + 2 * n_mix]
    rest = refs[4 + 2 * n_mix:]
    if final:
        gf_ref, o_ref, h_ref, acc_ref = rest
    else:
        o_ref, h_ref, acc_ref = rest
    x = x_ref[...]
    for m_ref, p_ref in zip(mix_refs, proj_refs):
        x = x + jnp.dot(m_ref[...], p_ref[...], preferred_element_type=F32)
    o_ref[...] = x
    h_ref[...] = _rms(x, g_ref[...]).astype(BF16)
    f = wo_ref.shape[0]
    for j in range(n_chunks):
        lo, hi = j * FFN_CHUNK, (j + 1) * FFN_CHUNK
        h = h_ref[...]
        a = jnp.dot(h, wi_ref[:, lo:hi], preferred_element_type=F32)
        b = jnp.dot(h, wi_ref[:, f + lo:f + hi], preferred_element_type=F32)
        act = (_silu(a) * b).astype(BF16)
        part = jnp.dot(act, wo_ref[lo:hi, :], preferred_element_type=F32)
        if j == 0:
            acc_ref[...] = part
        else:
            acc_ref[...] += part
    y = o_ref[...] + 0.5 * acc_ref[...]
    if final:
        y = _rms(y, gf_ref[...])
    o_ref[...] = y


def _ffn(x2, g, w_in, w_out, final_g=None, mixer=None):
    t, d = x2.shape
    f = w_out.shape[0]
    assert t % FFN_ROW_TILE == 0 and f % FFN_CHUNK == 0, (t, f)
    n_chunks = f // FFN_CHUNK
    wi = w_in.astype(BF16)
    wo = w_out.astype(BF16)
    final = final_g is not None
    row = lambda n: pl.BlockSpec((FFN_ROW_TILE, n), lambda i: (i, 0))
    in_specs = [row(d), _resident((1, d)), _resident(wi.shape), _resident(wo.shape)]
    args = [x2, g.reshape(1, d), wi, wo]
    outs, w_proj = mixer if mixer is not None else ((), None)
    lo = 0
    projs = []
    for o in outs:
        projs.append(w_proj[lo:lo + o.shape[1]].astype(BF16))
        lo += o.shape[1]
    in_specs += [row(o.shape[1]) for o in outs] + [_resident(p.shape) for p in projs]
    args += list(outs) + projs
    if final:
        in_specs.append(_resident((1, d)))
        args.append(final_g.reshape(1, d))
    return pl.pallas_call(
        functools.partial(_ffn_kernel, n_chunks=n_chunks, final=final, n_mix=len(outs)),
        grid=(t // FFN_ROW_TILE,),
        in_specs=in_specs,
        out_specs=row(d),
        out_shape=jax.ShapeDtypeStruct((t, d), F32),
        scratch_shapes=[pltpu.VMEM((FFN_ROW_TILE, d), BF16), pltpu.VMEM((FFN_ROW_TILE, d), F32)],
        compiler_params=_params("parallel"),
        name="ffn",
    )(*args)


def _inproj_kernel(*refs, big_chunks, with_c):
    if with_c:
        x_ref, g_ref, wbig_ref, wsm_ref, wc_ref, gc_ref, big_ref, sm_ref, c_ref = refs
    else:
        x_ref, g_ref, wbig_ref, wsm_ref, big_ref, sm_ref = refs
    h = _rms(x_ref[...], g_ref[...]).astype(BF16)
    for lo, hi in big_chunks:
        big_ref[:, lo:hi] = jnp.dot(h, wbig_ref[:, lo:hi], preferred_element_type=F32).astype(BF16)
    sm_ref[...] = jnp.dot(h, wsm_ref[...], preferred_element_type=F32)
    if with_c:
        ckv = jnp.dot(h, wc_ref[...], preferred_element_type=F32)
        c_ref[...] = _rms(ckv, gc_ref[...]).astype(BF16)


def _inproj(x2, g, w_big, w_small, w_c=None, g_c=None):
    t, d = x2.shape
    nbig = w_big.shape[1]
    assert t % ROW_TILE == 0 and nbig % LANES == 0, (t, nbig)
    with_c = w_c is not None
    step = 4 * LANES
    big_chunks = tuple((lo, min(lo + step, nbig)) for lo in range(0, nbig, step))
    row = lambda n: pl.BlockSpec((ROW_TILE, n), lambda i: (i, 0))
    in_specs = [row(d), _resident((1, d)), _resident(w_big.shape), _resident(w_small.shape)]
    args = [x2, g.reshape(1, d), w_big.astype(BF16), w_small.astype(BF16)]
    out_specs = [row(nbig), row(LANES)]
    out_shape = [jax.ShapeDtypeStruct((t, nbig), BF16), jax.ShapeDtypeStruct((t, LANES), F32)]
    if with_c:
        nc = w_c.shape[1]
        in_specs += [_resident(w_c.shape), _resident((1, nc))]
        args += [w_c.astype(BF16), g_c.reshape(1, nc)]
        out_specs.append(row(nc))
        out_shape.append(jax.ShapeDtypeStruct((t, nc), BF16))
    return pl.pallas_call(
        functools.partial(_inproj_kernel, big_chunks=big_chunks, with_c=with_c),
        grid=(t // ROW_TILE,),
        in_specs=in_specs,
        out_specs=out_specs,
        out_shape=out_shape,
        compiler_params=_params("parallel"),
        name="inproj_c" if with_c else "inproj",
    )(*args)


def _gla_kernel(q_ref, k_ref, v_ref, go_ref, sm_ref, wg_ref, bg_ref, ng_ref, o_ref, st_ref):
    rs = q_ref.shape[1]
    c_len = GLA_CHUNK

    @pl.when(pl.program_id(1) == 0)
    def _():
        st_ref[...] = jnp.zeros_like(st_ref)

    pre = jnp.dot(sm_ref[0].astype(BF16), wg_ref[...], preferred_element_type=F32) + bg_ref[...]
    gk = _log_sigmoid(pre) / GLA_GATE_NORMALIZER
    r = lax.broadcasted_iota(jnp.int32, (rs, rs), 0)
    c = lax.broadcasted_iota(jnp.int32, (rs, rs), 1)
    same = (r // c_len) == (c // c_len)
    tri = jnp.where(same & (c <= r), 1.0, 0.0).astype(F32)
    ones = jnp.where(same, 1.0, 0.0).astype(F32)
    g_cum = jnp.dot(tri, gk, precision=lax.Precision.HIGHEST, preferred_element_type=F32)
    g_last = jnp.dot(ones, gk, precision=lax.Precision.HIGHEST, preferred_element_type=F32)
    q = q_ref[0].astype(F32) * (GLA_DK ** -0.5)
    k = k_ref[0].astype(F32)
    q_dec = (q * jnp.exp(g_cum)).astype(BF16)
    k_dec = (k * jnp.exp(-g_cum)).astype(BF16)
    k_end = (k * jnp.exp(g_last - g_cum)).astype(BF16)
    decay = jnp.exp(g_last)

    lane = lax.broadcasted_iota(jnp.int32, (1, LANES), 1)
    ri = lax.broadcasted_iota(jnp.int32, (c_len, c_len), 0)
    ci = lax.broadcasted_iota(jnp.int32, (c_len, c_len), 1)
    causal = ci <= ri
    ng = ng_ref[...]
    zero = jnp.zeros((), BF16)
    for ch in range(rs // c_len):
        rows = slice(ch * c_len, (ch + 1) * c_len)
        for h in range(GLA_HEADS):
            pair = slice((h // 2) * LANES, (h // 2 + 1) * LANES)
            mine = (lane // GLA_DK) == (h % 2)
            vcol = slice(h * GLA_DV, (h + 1) * GLA_DV)
            qm = jnp.where(mine, q_dec[rows, pair], zero)
            a = lax.dot_general(qm, k_dec[rows, pair], _CONTRACT_LAST, preferred_element_type=F32)
            a = jnp.where(causal, a, 0.0)
            vh = v_ref[0, rows, vcol]
            st = st_ref[h]
            o = jnp.dot(a.astype(BF16), vh, preferred_element_type=F32)
            o = o + lax.dot_general(qm, st.astype(BF16), _CONTRACT_LAST, preferred_element_type=F32)
            km = jnp.where(mine, k_end[rows, pair], zero)
            upd = lax.dot_general(vh, km, _CONTRACT_FIRST, preferred_element_type=F32)
            st_ref[h] = st * decay[ch * c_len:ch * c_len + 1, pair] + upd
            on = _rms(o, ng)
            gate = go_ref[0, rows, vcol].astype(F32)
            o_ref[0, rows, vcol] = (on * _silu(gate)).astype(BF16)


def _gla(big, small, w_gate, b_gate, norm_g):
    b, s, _ = big.shape
    assert s % GLA_ROWS == 0, s
    hk = GLA_HEADS * GLA_DK
    hv = GLA_HEADS * GLA_DV
    wg =jnp.pad(w_gate, ((0, LANES - GLA_GATE_RANK), (0, 0))).astype(BF16)
    blk = lambda n, cb: pl.BlockSpec((1, GLA_ROWS, n), lambda bi, si: (bi, si, cb))
    return pl.pallas_call(
        _gla_kernel,
        grid=(b, s // GLA_ROWS),
        in_specs=[blk(hk, 0), blk(hk, 1), blk(hv, 1), blk(hv, 2), blk(LANES, 0),
                  _resident(wg.shape), _resident((1, hk)), _resident((1, GLA_DV))],
        out_specs=blk(hv, 0),
        out_shape=jax.ShapeDtypeStruct((b, s, hv), BF16),
        scratch_shapes=[pltpu.VMEM((GLA_HEADS, GLA_DV, LANES), F32)],
        compiler_params=_params("parallel", "arbitrary"),
        name="gla",
    )(big, big, big, big, small, wg, b_gate.reshape(1, hk), norm_g.reshape(1, GLA_DV))


def _foxgate_kernel(fl_ref, b_ref, f_ref):
    x = _log_sigmoid(fl_ref[0] + b_ref[...])
    s = x.shape[1]
    lane = lax.broadcasted_iota(jnp.int32, x.shape, 1)
    sh = 1
    while sh < s:
        x = x + jnp.where(lane >= sh, pltpu.roll(x, sh, 1), 0.0)
        sh *= 2
    f_ref[0] = x


def _fox_gate(f_logit_t, bias):
    b, h, s = f_logit_t.shape
    blk = pl.BlockSpec((1, h, s), lambda bi: (bi, 0, 0))
    return pl.pallas_call(
        _foxgate_kernel,
        grid=(b,),
        in_specs=[blk, _resident((h, 1))],
        out_specs=blk,
        out_shape=jax.ShapeDtypeStruct((b, h, s), F32),
        compiler_params=_params("parallel"),
        name="fox_gate",
    )(f_logit_t, bias.reshape(h, 1))


def _fox_kernel(q_ref, k_ref, v_ref, f_ref, fcol_ref, o_ref, qs_ref, mx_ref, l_ref, acc_ref, lsum_ref, kmax_ref):
    tq = q_ref.shape[1]
    tk = tq
    qi = pl.program_id(1)
    n_pairs = FOX_HEADS // 2
    lane = lax.broadcasted_iota(jnp.int32, (1, LANES), 1)
    zero = jnp.zeros((), BF16)

    @pl.when(qi == 0)
    def _():
        for pair in range(n_pairs):
            kf = k_ref[0, :, pair * LANES:(pair + 1) * LANES].astype(F32)
            kmax_ref[pair] = jnp.full(kmax_ref.shape[1:], jnp.sqrt(jnp.max(jnp.sum(kf * kf, axis=-1, keepdims=True))))

    f_rows = fcol_ref[0]
    for h in range(FOX_HEADS):
        rows = slice(h * tq, (h + 1) * tq)
        q = q_ref[0, :, (h // 2) * LANES:(h // 2 + 1) * LANES] * jnp.asarray(FOX_DH ** -0.5, BF16)
        q = jnp.where((lane // FOX_DH) == (h % 2), q, zero)
        qs_ref[rows, :] = q
        qf = q.astype(F32)
        n2 = jnp.dot((qf * qf).astype(BF16), jnp.ones((LANES, LANES), BF16), preferred_element_type=F32)
        bound = jnp.sqrt(n2) * (BOUND_SLACK * kmax_ref[h // 2, 0:1, 0:1]) - f_rows[:, h:h + 1]
        mx_ref[rows] = jnp.concatenate([bound] * (tk // LANES), axis=1)
    l_ref[...] = jnp.zeros_like(l_ref)
    acc_ref[...] = jnp.zeros_like(acc_ref)
    row = lax.broadcasted_iota(jnp.int32, (tq, tk), 0)
    col = lax.broadcasted_iota(jnp.int32, (tq, tk), 1)

    def logits(j, diagonal):
        start = pl.multiple_of(j * tk, tk)
        for pair in range(n_pairs):
            kb = k_ref[0, pl.ds(start, tk), pair * LANES:(pair + 1) * LANES]
            s_all = lax.dot_general(qs_ref[2 * pair * tq:2 * (pair + 1) * tq, :], kb, _CONTRACT_LAST,
                                    preferred_element_type=F32)
            for hh in range(2):
                h = 2 * pair + hh
                s = s_all[hh * tq:(hh + 1) * tq] - f_ref[0, h, pl.ds(j, 1), :]
                if diagonal:
                    s = jnp.where(col <= row, s, NEG_BIG)
                yield slice(h * tq, (h + 1) * tq), s, pair

    def max_block(j, diagonal):
        for rows, s, _ in logits(j, diagonal):
            mx_ref[rows] = jnp.maximum(mx_ref[rows], s)

    def sum_block(j, diagonal):
        start = pl.multiple_of(j * tk, tk)
        for rows, s, pair in logits(j, diagonal):
            p = jnp.exp(s - mx_ref[rows])
            l_ref[rows] += p
            vb = v_ref[0, pl.ds(start, tk), pair * LANES:(pair + 1) * LANES]
            acc_ref[rows] += jnp.dot(p.astype(BF16), vb, preferred_element_type=F32)

    def loop(fn):
        def body(j, carry):
            fn(j, False)
            return carry
        lax.fori_loop(0, qi, body, 0)
        fn(qi, True)

    def row_sums():
        smallest = None
        for h in range(FOX_HEADS):
            rows = slice(h * tq, (h + 1) * tq)
            ls = jnp.sum(l_ref[rows], axis=-1, keepdims=True)
            lsum_ref[rows] = ls
            smallest = ls if smallest is None else jnp.minimum(smallest, ls)
        return jnp.min(smallest)

    loop(sum_block)

    @pl.when(row_sums() < UNDERFLOW_GUARD)
    def _():
        mx_ref[...] = jnp.full_like(mx_ref, NEG_BIG)
        l_ref[...] = jnp.zeros_like(l_ref)
        acc_ref[...] = jnp.zeros_like(acc_ref)
        loop(max_block)
        for h in range(FOX_HEADS):
            rows = slice(h * tq, (h + 1) * tq)
            mx_ref[rows] = jnp.broadcast_to(jnp.max(mx_ref[rows], axis=-1, keepdims=True), (tq, tk))
        loop(sum_block)
        row_sums()

    for pair in range(n_pairs):
        o = [acc_ref[h * tq:(h + 1) * tq] / lsum_ref[h * tq:(h + 1) * tq] for h in (2 * pair, 2 * pair + 1)]
        o_ref[0, :, pair * LANES:(pair + 1) * LANES] = jnp.where((lane // FOX_DH) == 0, o[0], o[1]).astype(BF16)


def _fox(big, f_cum, q_col, k_col, v_col):
    b, s, _ = big.shape
    assert s % FOX_TQ == 0, s
    hf = FOX_HEADS * FOX_DH
    nkb = s // FOX_TQ
    f4 = f_cum.reshape(b, FOX_HEADS, nkb, FOX_TQ)
    rows = FOX_HEADS * FOX_TQ
    return pl.pallas_call(
        _fox_kernel,
        grid=(b, s // FOX_TQ),
        in_specs=[
            pl.BlockSpec((1, FOX_TQ, hf), lambda bi, qi: (bi, qi, q_col)),
            pl.BlockSpec((1, s, hf), lambda bi, qi: (bi, 0, k_col)),
            pl.BlockSpec((1, s, hf), lambda bi, qi: (bi, 0, v_col)),
            pl.BlockSpec((1, FOX_HEADS, nkb, FOX_TQ), lambda bi, qi: (bi, 0, 0, 0)),
            pl.BlockSpec((1, FOX_TQ, FOX_HEADS), lambda bi, qi: (bi, qi, 0)),
        ],
        out_specs=pl.BlockSpec((1, FOX_TQ, hf), lambda bi, qi: (bi, qi, 0)),
        out_shape=jax.ShapeDtypeStruct((b, s, hf), BF16),
        scratch_shapes=[pltpu.VMEM((rows, LANES), BF16),
                        pltpu.VMEM((rows, FOX_TQ), F32),
                        pltpu.VMEM((rows, FOX_TQ), F32),
                        pltpu.VMEM((rows, LANES), F32),
                        pltpu.VMEM((rows, 1), F32),
                        pltpu.VMEM((FOX_HEADS // 2, 8, LANES), F32)],
        compiler_params=_params("parallel", "arbitrary"),
        name="fox",
    )(big, big, big, f4, f_cum.transpose(0, 2, 1))


def _t5_bucket_np(dist):
    max_exact = T5_BUCKETS // 2
    d = np.maximum(dist, 1).astype(np.float32)
    large = max_exact + (np.log(d / np.float32(max_exact)) / np.float32(math.log(T5_MAX_DIST / max_exact))
                         * np.float32(T5_BUCKETS - max_exact)).astype(np.int32)
    large = np.minimum(large, T5_BUCKETS - 1)
    return np.where(dist < max_exact, dist, large).astype(np.int32)


N_BIAS_TILES = -(-(DSA_TK + T5_MAX_DIST - 1) // DSA_TQ)
LOG2E = math.log2(math.e)


def _bias_bucket_tiles():
    r = np.arange(DSA_TQ)[:, None]
    c = np.arange(DSA_TK)[None, :]
    return np.stack([_t5_bucket_np(np.maximum(t * DSA_TQ + r - c, 0)) for t in range(N_BIAS_TILES)])


def _bias_kernel(tab_ref, bkt_ref, o_ref):
    h = pl.program_id(1)
    bkt = bkt_ref[0]
    acc = jnp.zeros(bkt.shape, F32)
    for b in range(T5_BUCKETS):
        acc = jnp.where(bkt == b, tab_ref[b, h], acc)
    o_ref[0, 0] = (acc - tab_ref[T5_BUCKETS - 1, h]) * LOG2E


def _bias_tiles(t5_table):
    bkt = jnp.asarray(_bias_bucket_tiles())
    return pl.pallas_call(
        _bias_kernel,
        grid=(N_BIAS_TILES, DSA_HEADS),
        in_specs=[pl.BlockSpec(memory_space=pltpu.SMEM),
                  pl.BlockSpec((1, DSA_TQ, DSA_TK), lambda t, h: (t, 0, 0))],
        out_specs=pl.BlockSpec((1, 1, DSA_TQ, DSA_TK), lambda t, h: (t, h, 0, 0)),
        out_shape=jax.ShapeDtypeStruct((N_BIAS_TILES, DSA_HEADS, DSA_TQ, DSA_TK), F32),
        compiler_params=_params("parallel", "parallel"),
        name="t5_bias",
    )(t5_table, bkt)


def _dsa_kernel(q_ref, qi_ref, wt_ref, k2_ref, c_ref, wuk_ref, wuv_ref, bias_ref, tab_ref, o_ref,
                keyt_ref, hi_ref, lo_ref, selb_ref, qim_ref, qall_ref, p_ref, mx_ref, l_ref, acc_ref,
                lsum_ref, cmax_ref, *, k_top):
    tq, tk = DSA_TQ, DSA_TK
    sub = 8
    i = pl.program_id(1)
    nsb = (i * tq + tq + tk - 1) // tk
    lane = lax.broadcasted_iota(jnp.int32, (1, LANES), 1)
    qpos = i * tq + lax.broadcasted_iota(jnp.int32, (1, tq), 1)
    krow = lax.broadcasted_iota(jnp.int32, (tk, 1), 0)
    zero = jnp.zeros((), BF16)

    w_s = wt_ref[0] * ((IDX_HEADS ** -0.5) * (IDX_DIM ** -0.5))
    for h in range(IDX_HEADS):
        qp = qi_ref[0, :, (h // 2) * LANES:(h // 2 + 1) * LANES]
        qim_ref[h] = jnp.where((lane // IDX_DIM) == (h % 2), qp, zero)

    def score_block(j):
        start = pl.multiple_of(j * tk, tk)
        kb = k2_ref[0, pl.ds(start, tk), :]
        sc = jnp.zeros((tk, tq), F32)
        for h in range(IDX_HEADS):
            lg = lax.dot_general(kb, qim_ref[h], _CONTRACT_LAST, preferred_element_type=F32)
            sc = sc + w_s[h:h + 1, :] * jnp.maximum(lg, 0.0)
        sc = jnp.where(sc == 0.0, 0.0, sc)
        bits = pltpu.bitcast(sc, jnp.int32)
        key = bits ^ ((bits >> 31) & 0x7FFFFFFF)
        key = jnp.where(start + krow <= qpos, key, KEY_NEG_INF)
        keyt_ref[j] = key
        hi_ref[j] = (key >> 16).astype(jnp.int16)
        lo_ref[j] = ((key & 0xFFFF) - HALF).astype(jnp.int16)

    def score_pair(t, carry):
        score_block(2 * t)
        score_block(2 * t + 1)
        return carry

    def score_last(j, carry):
        score_block(j)
        return carry

    lax.fori_loop(0, nsb // 2, score_pair, 0)
    lax.fori_loop((nsb // 2) * 2, nsb, score_last, 0)
    lowest = jnp.full((tk, tq), -HALF, jnp.int16)
    for pad in range(COUNT_STEP - 1):
        hi_ref[nsb + pad] = lowest
        lo_ref[nsb + pad] = lowest
    keyt_ref[nsb] = jnp.full((tk, tq), INT_MIN, jnp.int32)

    def tree_sum(x, rows):
        parts = [x[r * rows:(r + 1) * rows] for r in range(x.shape[0] // rows)]
        while len(parts) > 1:
            parts = [a + b for a, b in zip(parts[::2], parts[1::2])]
        return parts[0]

    def count(pred):
        def body(j, acc):
            return acc + tree_sum(jnp.where(pred(keyt_ref[j], j), 1.0, 0.0), sub)
        per_sublane = lax.fori_loop(0, nsb, body, jnp.zeros((sub, tq), F32))
        return jnp.sum(per_sublane, axis=0, keepdims=True)

    one16, zero16 = jnp.ones((), jnp.int16), jnp.zeros((), jnp.int16)

    def count16(ref, pred):
        def body(t, acc):
            x = jnp.where(pred(ref[COUNT_STEP * t]), one16, zero16)
            for b in range(1, COUNT_STEP):
                x = x + jnp.where(pred(ref[COUNT_STEP * t + b]), one16, zero16)
            return acc + tree_sum(x, 2 * sub)
        per_row = lax.fori_loop(0, (nsb + COUNT_STEP - 1) // COUNT_STEP, body,
                                jnp.zeros((2 * sub, tq), jnp.int16))
        return jnp.sum(per_row.astype(jnp.int32), axis=0, keepdims=True)

    def search16(ref, want):
        def step(t, u):
            cand_u = u | lax.shift_left(jnp.int32(1), 15 - t)
            cand = (cand_u - HALF).astype(jnp.int16)
            n = count16(ref, lambda v: v >= cand)
            return jnp.where(n >= want, cand_u, u)
        return lax.fori_loop(0, 16, step, jnp.zeros((1, tq), jnp.int32)) - HALF

    t_hi = search16(hi_ref, k_top)
    t_hi16 = t_hi.astype(jnp.int16)

    def above_and_ties(t, acc):
        for j in (2 * t, 2 * t + 1):
            hi = hi_ref[j]
            acc = acc + tree_sum(jnp.where(hi > t_hi16, one16, zero16), 2 * sub)
            lo_ref[j] = jnp.where(hi == t_hi16, lo_ref[j], jnp.asarray(-HALF, jnp.int16))
        return acc

    per_row = lax.fori_loop(0, (nsb + 1) // 2, above_and_ties, jnp.zeros((2 * sub, tq), jnp.int16))
    n_above = jnp.sum(per_row.astype(jnp.int32), axis=0, keepdims=True)
    t_lo = search16(lo_ref, k_top - n_above)
    thr = lax.shift_left(t_hi, 16) | (t_lo + HALF)

    def gt_and_ge(t, accs):
        gt, ge = accs
        for j in (2 * t, 2 * t + 1):
            kk = keyt_ref[j]
            gt = gt + tree_sum(jnp.where(kk > thr, 1.0, 0.0), sub)
            ge = ge + tree_sum(jnp.where(kk >= thr, 1.0, 0.0), sub)
        return gt, ge

    zeros = jnp.zeros((sub, tq), F32)
    gt, ge = lax.fori_loop(0, (nsb + 1) // 2, gt_and_ge, (zeros, zeros))
    n_gt = jnp.sum(gt, axis=0, keepdims=True)
    n_ge = jnp.sum(ge, axis=0, keepdims=True)
    need = k_top - n_gt
    excess = (n_ge > k_top) & (thr > KEY_NEG_INF)
    s_total = k2_ref.shape[1]

    def tie_search():
        def step(t, p):
            cand = p | lax.shift_left(jnp.int32(1), (s_total.bit_length() - 1) - t)
            n = count(lambda kk, j: (kk == thr) & (j * tk + krow <= cand - 1))
            return jnp.where(n < need, cand, p)
        p = lax.fori_loop(0, s_total.bit_length(), step, jnp.zeros((1, tq), jnp.int32))
        return jnp.where(excess, p, s_total)

    any_excess = jnp.max(jnp.where(excess, 1.0, 0.0)) > 0.0
    tie_limit = lax.cond(any_excess, tie_search, lambda: jnp.full((1, tq), s_total, jnp.int32))

    def select_blocks(t, carry):
        for j in (2 * t, 2 * t + 1):
            key = keyt_ref[j]
            kpos = j * tk + krow
            sel = ((key > thr) | ((key == thr) & (kpos <= tie_limit))) & (kpos <= qpos)
            selb_ref[j] = jnp.where(sel, 0.0, NEG_BIG).T
        return carry

    lax.fori_loop(0, (nsb + 1) // 2, select_blocks, 0)

    @pl.when(i == 0)
    def _():
        cf = c_ref[0].astype(F32)
        cmax_ref[...] = jnp.full(cmax_ref.shape, jnp.sqrt(jnp.max(jnp.sum(cf * cf, axis=-1, keepdims=True))))

    c_norm = cmax_ref[0:1, 0:1]
    tab = tab_ref[...]
    bias_max = (jnp.max(tab, axis=0, keepdims=True) - tab[T5_BUCKETS - 1:T5_BUCKETS, :]) * LOG2E
    for h in range(DSA_HEADS):
        rows = slice(h * tq, (h + 1) * tq)
        qp = q_ref[0, :, (h // 2) * LANES:(h // 2 + 1) * LANES]
        ql = jnp.dot(qp, wuk_ref[h], preferred_element_type=F32) * ((DSA_DH ** -0.5) * LOG2E)
        ql = ql.astype(BF16)
        qall_ref[rows, :] = ql
        qf = ql.astype(F32)
        n2 = jnp.dot((qf * qf).astype(BF16), jnp.ones((DSA_LATENT, LANES), BF16), preferred_element_type=F32)
        bound = jnp.sqrt(n2) * (BOUND_SLACK * c_norm) + bias_max[:, h:h + 1]
        mx_ref[rows] = jnp.concatenate([bound] * (tk // LANES), axis=1)
    l_ref[...] = jnp.zeros_like(l_ref)
    acc_ref[...] = jnp.zeros_like(acc_ref)
    n_far = jnp.maximum(((i - N_BIAS_TILES) * tq) // tk + 1, 0)

    def logits(j0, nblk, near):
        rc = tq // 2
        start = pl.multiple_of(j0 * tk, tk)
        cb = c_ref[0, pl.ds(start, nblk * tk), :]
        s_all = lax.dot_general(qall_ref[...], cb, _CONTRACT_LAST, preferred_element_type=F32)
        for h in range(DSA_HEADS):
            for r0 in range(0, tq, rc):
                rows = slice(h * tq + r0, h * tq + r0 + rc)
                tiles = []
                for b in range(nblk):
                    s = s_all[rows, b * tk:(b + 1) * tk] + selb_ref[j0 + b, r0:r0 + rc, :]
                    if near:
                        s = s + bias_ref[(i * tq - (j0 + b) * tk) // tq, h, r0:r0 + rc, :]
                    tiles.append(s)
                yield rows, tiles, cb

    def max_blocks(j0, nblk, near):
        for rows, tiles, _ in logits(j0, nblk, near):
            m = mx_ref[rows]
            for s in tiles:
                m = jnp.maximum(m, s)
            mx_ref[rows] = m

    def sum_blocks(j0, nblk, near):
        for rows, tiles, cb in logits(j0, nblk, near):
            m = mx_ref[rows]
            l = l_ref[rows]
            for b, s in enumerate(tiles):
                p = jnp.exp2(s - m)
                l = l + p
                p_ref[rows, b * tk:(b + 1) * tk] = p.astype(BF16)
            l_ref[rows] = l
        acc_ref[...] += jnp.dot(p_ref[:, :nblk * tk], cb, preferred_element_type=F32)

    def sweep(fn):
        def loop(lo, hi, stride, near):
            def body(t, carry):
                fn(lo + t * stride, stride, near)
                return carry
            lax.fori_loop(0, (hi - lo) // stride, body, 0)
        lo, stride = 0, FAR_STEP
        while stride >= 1:
            hi = lo + ((n_far - lo) // stride) * stride
            loop(lo, hi, stride, False)
            lo, stride = hi, stride // 2
        loop(n_far, nsb, 1, True)

    def row_sums():
        smallest = None
        for h in range(DSA_HEADS):
            rows = slice(h * tq, (h + 1) * tq)
            ls = jnp.sum(l_ref[rows], axis=-1, keepdims=True)
            lsum_ref[rows] = ls
            smallest = ls if smallest is None else jnp.minimum(smallest, ls)
        return jnp.min(smallest)

    sweep(sum_blocks)

    @pl.when(row_sums() < UNDERFLOW_GUARD)
    def _():
        mx_ref[...] = jnp.full_like(mx_ref, NEG_BIG)
        l_ref[...] = jnp.zeros_like(l_ref)
        acc_ref[...] = jnp.zeros_like(acc_ref)
        sweep(max_blocks)
        for h in range(DSA_HEADS):
            rows = slice(h * tq, (h + 1) * tq)
            mx_ref[rows] = jnp.broadcast_to(jnp.max(mx_ref[rows], axis=-1, keepdims=True), (tq, tk))
        sweep(sum_blocks)
        row_sums()

    for pair in range(DSA_HEADS // 2):
        out = jnp.zeros((tq, LANES), F32)
        for h in (2 * pair, 2 * pair + 1):
            rows = slice(h * tq, (h + 1) * tq)
            o_lat = (acc_ref[rows] / lsum_ref[rows]).astype(BF16)
            out = out + jnp.dot(o_lat, wuv_ref[h], preferred_element_type=F32)
        o_ref[0, :, pair * LANES:(pair + 1) * LANES] = out.astype(BF16)


def _dsa(big, w_idx_t, c, w_uk, w_uv, bias_tiles, t5_table, q_col, qi_col, k2_col):
    b, s, _ = big.shape
    assert s % DSA_TK == 0 and DSA_TK % DSA_TQ == 0, s
    k_top = min(TOPK_MAX, s // 4)
    hq = DSA_HEADS * DSA_DH
    hi = IDX_HEADS * IDX_DIM
    mine = (np.arange(LANES)[None, :] // DSA_DH) == (np.arange(DSA_HEADS)[:, None] % 2)
    wuk = jnp.where(mine[:, :, None], jnp.concatenate([w_uk, w_uk], axis=1), 0.0).astype(BF16)
    wuv = jnp.where(mine[:, None, :], jnp.concatenate([w_uv, w_uv], axis=2), 0.0).astype(BF16)
    nq = s // DSA_TQ
    rows = DSA_HEADS * DSA_TQ
    return pl.pallas_call(
        functools.partial(_dsa_kernel, k_top=k_top),
        grid=(b, nq),
        in_specs=[
            pl.BlockSpec((1, DSA_TQ, hq), lambda bi, qi: (bi, qi, q_col)),
            pl.BlockSpec((1, DSA_TQ, hi), lambda bi, qi: (bi, qi, qi_col)),
            pl.BlockSpec((1, IDX_HEADS, DSA_TQ), lambda bi, qi: (bi, 0, qi)),
            pl.BlockSpec((1, s, LANES), lambda bi, qi: (bi, 0, k2_col)),
            pl.BlockSpec((1, s, DSA_LATENT), lambda bi, qi: (bi, 0, 0)),
            _resident(wuk.shape), _resident(wuv.shape), _resident(bias_tiles.shape), _resident(t5_table.shape),
        ],
        out_specs=pl.BlockSpec((1, DSA_TQ, hq), lambda bi, qi: (bi, qi, 0)),
        out_shape=jax.ShapeDtypeStruct((b, s, hq), BF16),
        scratch_shapes=[
            pltpu.VMEM((s // DSA_TK + 1, DSA_TK, DSA_TQ), jnp.int32),
            pltpu.VMEM((s // DSA_TK + COUNT_STEP - 1, DSA_TK, DSA_TQ), jnp.int16),
            pltpu.VMEM((s // DSA_TK + COUNT_STEP - 1, DSA_TK, DSA_TQ), jnp.int16),
            pltpu.VMEM((s // DSA_TK + 1, DSA_TQ, DSA_TK), F32),
            pltpu.VMEM((IDX_HEADS, DSA_TQ, LANES), BF16),
            pltpu.VMEM((rows, DSA_LATENT), BF16),
            pltpu.VMEM((rows, FAR_STEP * DSA_TK), BF16),
            pltpu.VMEM((rows, DSA_TK), F32),
            pltpu.VMEM((rows, DSA_TK), F32),
            pltpu.VMEM((rows, DSA_LATENT), F32),
            pltpu.VMEM((rows, 1), F32),
            pltpu.VMEM((8, LANES), F32),
        ],
        compiler_params=_params("parallel", "arbitrary"),
        name="dsa",
    )(big, big, w_idx_t, big, c, wuk, wuv, bias_tiles, t5_table)


def _even_mixer(x2, b, s, g, w_in, w_gate, b_gate, norm_g, fox_b):
    hk = GLA_HEADS * GLA_DK
    hv = GLA_HEADS * GLA_DV
    hf = FOX_HEADS * FOX_DH
    o = np.cumsum([0, hk, hk, hv, hv, GLA_GATE_RANK, hf, hf, hf, FOX_HEADS])
    col = lambda n: w_in[:, o[n]:o[n + 1]]
    w_big = jnp.concatenate([col(0), col(1), col(2), col(3), col(5), col(6), col(7)], axis=1)
    w_small = jnp.pad(jnp.concatenate([col(4), col(8)], axis=1), ((0, 0), (0, LANES - GLA_GATE_RANK - FOX_HEADS)))
    big, small = _inproj(x2, g, w_big, w_small)
    big = big.reshape(b, s, -1)
    small = small.reshape(b, s, LANES)
    o_gla = _gla(big, small, w_gate, b_gate, norm_g)
    f_logit_t = small[:, :, GLA_GATE_RANK:GLA_GATE_RANK + FOX_HEADS].transpose(0, 2, 1)
    f_cum = _fox_gate(f_logit_t, fox_b)
    fox_col = (2 * hk + 2 * hv) // hf
    o_fox = _fox(big, f_cum, fox_col, fox_col + 1, fox_col + 2)
    return [o_gla.reshape(b * s, hv), o_fox.reshape(b * s, hf)]


def _odd_mixer(x2, b, s, g, w_in, kv_g, w_uk, w_uv, bias_tiles, t5_table):
    hq = DSA_HEADS * DSA_DH
    hi = IDX_HEADS * IDX_DIM
    o = np.cumsum([0, hq, DSA_LATENT, hi, IDX_DIM, IDX_HEADS])
    col = lambda n: w_in[:, o[n]:o[n + 1]]
    w_big = jnp.concatenate([col(0), col(2), col(3), col(3)], axis=1)
    w_small = jnp.pad(col(4), ((0, 0), (0, LANES - IDX_HEADS)))
    big, small, c = _inproj(x2, g, w_big, w_small, col(1), kv_g)
    big = big.reshape(b, s, -1)
    w_idx_t = small.reshape(b, s, LANES)[:, :, :IDX_HEADS].transpose(0, 2, 1)
    o_dsa = _dsa(big, w_idx_t, c.reshape(b, s, DSA_LATENT), w_uk, w_uv, bias_tiles, t5_table,
                 0, hq // hi, (hq + hi) // LANES)
    return [o_dsa.reshape(b * s, hq)]


def kernel(x, norm_g, ffn_w_in, ffn_w_out, even_w_in, gla_w_gate, gla_b_gate, gla_norm_g, fox_b_f, even_w_out,
           odd_w_in, mla_kv_norm_g, mla_w_uk, mla_w_uv, odd_w_out, t5_table, final_norm_g):
    b, s, d = x.shape
    depth = norm_g.shape[0]
    x2 = x.reshape(b * s, d)
    bias_tiles = _bias_tiles(t5_table) if depth > 1 else None
    for layer in range(depth):
        g = norm_g[layer]
        j = layer // 2
        x2 = _ffn(x2, g[0], ffn_w_in[layer, 0], ffn_w_out[layer, 0])
        if layer % 2 == 0:
            heads = _even_mixer(x2, b, s, g[1], even_w_in[j], gla_w_gate[j], gla_b_gate[j], gla_norm_g[j], fox_b_f[j])
            w_proj = even_w_out[j]
        else:
            heads = _odd_mixer(x2, b, s, g[1], odd_w_in[j], mla_kv_norm_g[j], mla_w_uk[j], mla_w_uv[j],
                               bias_tiles, t5_table)
            w_proj = odd_w_out[j]
        last = layer == depth - 1
        x2 = _ffn(x2, g[2], ffn_w_in[layer, 1], ffn_w_out[layer, 1], final_norm_g if last else None,
                  mixer=(heads, w_proj))
    return x2.reshape(b, s, d)
```

```python
import functools
import math

import numpy as np
import jax
import jax.numpy as jnp
from jax import lax
from jax.experimental import pallas as pl
from jax.experimental.pallas import tpu as pltpu

F32 = jnp.float32
BF16 = jnp.bfloat16

EPS = 1e-6
GLA_HEADS = 4
GLA_DK = 64
GLA_DV = 128
GLA_GATE_RANK = 16
GLA_GATE_NORMALIZER = 16.0
GLA_CHUNK = 64
FOX_HEADS = 8
FOX_DH = 64
DSA_HEADS = 16
DSA_DH = 64
DSA_LATENT = 256
IDX_HEADS = 8
IDX_DIM = 64
TOPK_MAX = 256
T5_BUCKETS = 32
T5_MAX_DIST = 128

LANES = 128
VMEM_LIMIT = 56 * 1024 * 1024

ROW_TILE = 512
FFN_ROW_TILE = 1024
FFN_CHUNK = 256
GLA_ROWS = 256
GLA_BATCH = 2
FOX_TQ = 512
DSA_TQ = 128
DSA_TK = 256
FAR_STEP = 4
COUNT_STEP = 4
NEG_BIG = -1e30
UNDERFLOW_GUARD = 2.0 ** -60
BOUND_SLACK = 1.01
INT_MIN = -2 ** 31
KEY_NEG_INF = INT_MIN + 0x7FFFFF
HALF = 1 << 15

_CONTRACT_LAST = (((1,), (1,)), ((), ()))
_CONTRACT_FIRST = (((0,), (0,)), ((), ()))


def _params(*sem):
    return pltpu.CompilerParams(dimension_semantics=sem, vmem_limit_bytes=VMEM_LIMIT)


def _resident(shape):
    nd = len(shape)
    return pl.BlockSpec(shape, lambda *_: (0,) * nd, pipeline_mode=pl.Buffered(1))


def _rms(x, g):
    return x * lax.rsqrt(jnp.mean(x * x, axis=-1, keepdims=True) + EPS) * g


def _log_sigmoid(x):
    return jnp.minimum(x, 0.0) - jnp.log1p(jnp.exp(-jnp.abs(x)))


def _silu(x):
    return x * jax.nn.sigmoid(x)


def _ffn_kernel(*refs, n_chunks, final, n_mix):
    x_ref, g_ref, wi_ref, wo_ref = refs[:4]
    mix_refs = refs[4:4 + n_mix]
    proj_refs = refs[4 + n_mix:4 + 2 * n_mix]
    rest = refs[4 + 2 * n_mix:]
    if final:
        gf_ref, o_ref, h_ref, acc_ref = rest
    else:
        o_ref, h_ref, acc_ref = rest
    x = x_ref[...]
    for m_ref, p_ref in zip(mix_refs, proj_refs):
        x = x + jnp.dot(m_ref[...], p_ref[...], preferred_element_type=F32)
    o_ref[...] = x
    h_ref[...] = _rms(x, g_ref[...]).astype(BF16)
    f = wo_ref.shape[0]
    for j in range(n_chunks):
        lo, hi = j * FFN_CHUNK, (j + 1) * FFN_CHUNK
        h = h_ref[...]
        a = jnp.dot(h, wi_ref[:, lo:hi], preferred_element_type=F32)
        b = jnp.dot(h, wi_ref[:, f + lo:f + hi], preferred_element_type=F32)
        act = (_silu(a) * b).astype(BF16)
        part = jnp.dot(act, wo_ref[lo:hi, :], preferred_element_type=F32)
        if j == 0:
            acc_ref[...] = part
        else:
            acc_ref[...] += part
    y = o_ref[...] + 0.5 * acc_ref[...]
    if final:
        y = _rms(y, gf_ref[...])
    o_ref[...] = y


def _ffn(x2, g, w_in, w_out, final_g=None, mixer=None):
    t, d = x2.shape
    f = w_out.shape[0]
    assert t % FFN_ROW_TILE == 0 and f % FFN_CHUNK == 0, (t, f)
    n_chunks = f // FFN_CHUNK
    wi = w_in.astype(BF16)
    wo = w_out.astype(BF16)
    final = final_g is not None
    row = lambda n: pl.BlockSpec((FFN_ROW_TILE, n), lambda i: (i, 0))
    in_specs = [row(d), _resident((1, d)), _resident(wi.shape), _resident(wo.shape)]
    args = [x2, g.reshape(1, d), wi, wo]
    outs, w_proj = mixer if mixer is not None else ((), None)
    lo = 0
    projs = []
    for o in outs:
        projs.append(w_proj[lo:lo + o.shape[1]].astype(BF16))
        lo += o.shape[1]
    in_specs += [row(o.shape[1]) for o in outs] + [_resident(p.shape) for p in projs]
    args += list(outs) + projs
    if final:
        in_specs.append(_resident((1, d)))
        args.append(final_g.reshape(1, d))
    return pl.pallas_call(
        functools.partial(_ffn_kernel, n_chunks=n_chunks, final=final, n_mix=len(outs)),
        grid=(t // FFN_ROW_TILE,),
        in_specs=in_specs,
        out_specs=row(d),
        out_shape=jax.ShapeDtypeStruct((t, d), F32),
        scratch_shapes=[pltpu.VMEM((FFN_ROW_TILE, d), BF16), pltpu.VMEM((FFN_ROW_TILE, d), F32)],
        compiler_params=_params("parallel"),
        name="ffn",
    )(*args)


def _inproj_kernel(*refs, big_chunks, with_c):
    if with_c:
        x_ref, g_ref, wbig_ref, wsm_ref, wc_ref, gc_ref, big_ref, sm_ref, c_ref = refs
    else:
        x_ref, g_ref, wbig_ref, wsm_ref, big_ref, sm_ref = refs
    h = _rms(x_ref[...], g_ref[...]).astype(BF16)
    for lo, hi in big_chunks:
        big_ref[:, lo:hi] = jnp.dot(h, wbig_ref[:, lo:hi], preferred_element_type=F32).astype(BF16)
    sm_ref[...] = jnp.dot(h, wsm_ref[...], preferred_element_type=F32)
    if with_c:
        ckv = jnp.dot(h, wc_ref[...], preferred_element_type=F32)
        c_ref[...] = _rms(ckv, gc_ref[...]).astype(BF16)


def _inproj(x2, g, w_big, w_small, w_c=None, g_c=None):
    t, d = x2.shape
    nbig = w_big.shape[1]
    assert t % ROW_TILE == 0 and nbig % LANES == 0, (t, nbig)
    with_c = w_c is not None
    step = 4 * LANES
    big_chunks = tuple((lo, min(lo + step, nbig)) for lo in range(0, nbig, step))
    row = lambda n: pl.BlockSpec((ROW_TILE, n), lambda i: (i, 0))
    in_specs = [row(d), _resident((1, d)), _resident(w_big.shape), _resident(w_small.shape)]
    args = [x2, g.reshape(1, d), w_big.astype(BF16), w_small.astype(BF16)]
    out_specs = [row(nbig), row(LANES)]
    out_shape = [jax.ShapeDtypeStruct((t, nbig), BF16), jax.ShapeDtypeStruct((t, LANES), F32)]
    if with_c:
        nc = w_c.shape[1]
        in_specs += [_resident(w_c.shape), _resident((1, nc))]
        args += [w_c.astype(BF16), g_c.reshape(1, nc)]
        out_specs.append(row(nc))
        out_shape.append(jax.ShapeDtypeStruct((t, nc), BF16))
    return pl.pallas_call(
        functools.partial(_inproj_kernel, big_chunks=big_chunks, with_c=with_c),
        grid=(t // ROW_TILE,),
        in_specs=in_specs,
        out_specs=out_specs,
        out_shape=out_shape,
        compiler_params=_params("parallel"),
        name="inproj_c" if with_c else "inproj",
    )(*args)


def _gla_kernel(q_ref, k_ref, v_ref, go_ref, sm_ref, wg_ref, bg_ref, ng_ref, o_ref, st_ref):
    nb, rs = q_ref.shape[0], q_ref.shape[1]
    c_len = GLA_CHUNK

    @pl.when(pl.program_id(1) == 0)
    def _():
        st_ref[...] = jnp.zeros_like(st_ref)

    r = lax.broadcasted_iota(jnp.int32, (rs, rs), 0)
    c = lax.broadcasted_iota(jnp.int32, (rs, rs), 1)
    tri = jnp.where(((r // c_len) == (c // c_len)) & (c <= r), 1.0, 0.0).astype(BF16)
    q_dec, k_dec, k_end, decay = [], [], [], []
    for bb in range(nb):
        pre = jnp.dot(sm_ref[bb].astype(BF16), wg_ref[...], preferred_element_type=F32) + bg_ref[...]
        gk = _log_sigmoid(pre) / GLA_GATE_NORMALIZER
        g_cum = jnp.zeros_like(gk)
        rest = gk
        for _ in range(3):
            piece = rest.astype(BF16)
            g_cum = g_cum + jnp.dot(tri, piece, preferred_element_type=F32)
            rest = rest - piece.astype(F32)
        g_last = jnp.concatenate(
            [jnp.broadcast_to(g_cum[ch * c_len + c_len - 1:(ch + 1) * c_len, :], (c_len, g_cum.shape[1]))
             for ch in range(rs // c_len)], axis=0)
        q = q_ref[bb].astype(F32) * (GLA_DK ** -0.5)
        k = k_ref[bb].astype(F32)
        q_dec.append((q * jnp.exp(g_cum)).astype(BF16))
        k_dec.append((k * jnp.exp(-g_cum)).astype(BF16))
        k_end.append((k * jnp.exp(g_last - g_cum)).astype(BF16))
        decay.append(jnp.exp(g_last))

    lane = lax.broadcasted_iota(jnp.int32, (1, LANES), 1)
    ri = lax.broadcasted_iota(jnp.int32, (c_len, c_len), 0)
    ci = lax.broadcasted_iota(jnp.int32, (c_len, c_len), 1)
    causal = ci <= ri
    ng = ng_ref[...]
    zero = jnp.zeros((), BF16)
    for ch in range(rs // c_len):
        rows = slice(ch * c_len, (ch + 1) * c_len)
        for h in range(GLA_HEADS):
            pair = slice((h // 2) * LANES, (h // 2 + 1) * LANES)
            mine = (lane // GLA_DK) == (h % 2)
            vcol = slice(h * GLA_DV, (h + 1) * GLA_DV)
            for bb in range(nb):
                qm = jnp.where(mine, q_dec[bb][rows, pair], zero)
                a = lax.dot_general(qm, k_dec[bb][rows, pair], _CONTRACT_LAST, preferred_element_type=F32)
                a = jnp.where(causal, a, 0.0)
                vh = v_ref[bb, rows, vcol]
                st = st_ref[bb, h]
                o = jnp.dot(a.astype(BF16), vh, preferred_element_type=F32)
                o = o + lax.dot_general(qm, st.astype(BF16), _CONTRACT_LAST, preferred_element_type=F32)
                km = jnp.where(mine, k_end[bb][rows, pair], zero)
                upd = lax.dot_general(vh, km, _CONTRACT_FIRST, preferred_element_type=F32)
                st_ref[bb, h] = st * decay[bb][ch * c_len:ch * c_len + 1, pair] + upd
                on = _rms(o, ng)
                gate = go_ref[bb, rows, vcol].astype(F32)
                o_ref[bb, rows, vcol] = (on * _silu(gate)).astype(BF16)


def _gla(big, small, w_gate, b_gate, norm_g):
    b, s, _ = big.shape
    nb = GLA_BATCH if b % GLA_BATCH == 0 else 1
    assert s % GLA_ROWS == 0, s
    hk = GLA_HEADS * GLA_DK
    hv = GLA_HEADS * GLA_DV
    wg = jnp.pad(w_gate, ((0, LANES - GLA_GATE_RANK), (0, 0))).astype(BF16)
    blk = lambda n, cb: pl.BlockSpec((nb, GLA_ROWS, n), lambda bi, si: (bi, si, cb))
    return pl.pallas_call(
        _gla_kernel,
        grid=(b // nb, s // GLA_ROWS),
        in_specs=[blk(hk, 0), blk(hk, 1), blk(hv, 1), blk(hv, 2), blk(LANES, 0),
                  _resident(wg.shape), _resident((1, hk)), _resident((1, GLA_DV))],
        out_specs=blk(hv, 0),
        out_shape=jax.ShapeDtypeStruct((b, s, hv), BF16),
        scratch_shapes=[pltpu.VMEM((nb, GLA_HEADS, GLA_DV, LANES), F32)],
        compiler_params=_params("parallel", "arbitrary"),
        name="gla",
    )(big, big, big, big, small, wg, b_gate.reshape(1, hk), norm_g.reshape(1, GLA_DV))


def _foxgate_kernel(fl_ref, b_ref, f_ref):
    x = _log_sigmoid(fl_ref[0] + b_ref[...])
    s = x.shape[1]
    lane = lax.broadcasted_iota(jnp.int32, x.shape, 1)
    sh = 1
    while sh < s:
        x = x + jnp.where(lane >= sh, pltpu.roll(x, sh, 1), 0.0)
        sh *= 2
    f_ref[0] = x


def _fox_gate(f_logit_t, bias):
    b, h, s = f_logit_t.shape
    blk = pl.BlockSpec((1, h, s), lambda bi: (bi, 0, 0))
    return pl.pallas_call(
        _foxgate_kernel,
        grid=(b,),
        in_specs=[blk, _resident((h, 1))],
        out_specs=blk,
        out_shape=jax.ShapeDtypeStruct((b, h, s), F32),
        compiler_params=_params("parallel"),
        name="fox_gate",
    )(f_logit_t, bias.reshape(h, 1))


def _fox_kernel(q_ref, k_ref, v_ref, f_ref, fcol_ref, o_ref, qs_ref, mx_ref, l_ref, acc_ref, lsum_ref, kmax_ref):
    tq = q_ref.shape[1]
    tk = tq
    qi = pl.program_id(1)
    n_pairs = FOX_HEADS // 2
    lane = lax.broadcasted_iota(jnp.int32, (1, LANES), 1)
    zero = jnp.zeros((), BF16)

    @pl.when(qi == 0)
    def _():
        for pair in range(n_pairs):
            kf = k_ref[0, :, pair * LANES:(pair + 1) * LANES].astype(F32)
            kmax_ref[pair] = jnp.full(kmax_ref.shape[1:], jnp.sqrt(jnp.max(jnp.sum(kf * kf, axis=-1, keepdims=True))))

    f_rows = fcol_ref[0]
    for h in range(FOX_HEADS):
        rows = slice(h * tq, (h + 1) * tq)
        q = q_ref[0, :, (h // 2) * LANES:(h // 2 + 1) * LANES] * jnp.asarray(FOX_DH ** -0.5, BF16)
        q = jnp.where((lane // FOX_DH) == (h % 2), q, zero)
        qs_ref[rows, :] = q
        qf = q.astype(F32)
        n2 = jnp.dot((qf * qf).astype(BF16), jnp.ones((LANES, LANES), BF16), preferred_element_type=F32)
        bound = jnp.sqrt(n2) * (BOUND_SLACK * kmax_ref[h // 2, 0:1, 0:1]) - f_rows[:, h:h + 1]
        mx_ref[rows] = jnp.concatenate([bound] * (tk // LANES), axis=1)
    l_ref[...] = jnp.zeros_like(l_ref)
    acc_ref[...] = jnp.zeros_like(acc_ref)
    row = lax.broadcasted_iota(jnp.int32, (tq, tk), 0)
    col = lax.broadcasted_iota(jnp.int32, (tq, tk), 1)

    def logits(j, diagonal):
        start = pl.multiple_of(j * tk, tk)
        for pair in range(n_pairs):
            kb = k_ref[0, pl.ds(start, tk), pair * LANES:(pair + 1) * LANES]
            s_all = lax.dot_general(qs_ref[2 * pair * tq:2 * (pair + 1) * tq, :], kb, _CONTRACT_LAST,
                                    preferred_element_type=F32)
            for hh in range(2):
                h = 2 * pair + hh
                s = s_all[hh * tq:(hh + 1) * tq] - f_ref[0, h, pl.ds(j, 1), :]
                if diagonal:
                    s = jnp.where(col <= row, s, NEG_BIG)
                yield slice(h * tq, (h + 1) * tq), s, pair

    def max_block(j, diagonal):
        for rows, s, _ in logits(j, diagonal):
            mx_ref[rows] = jnp.maximum(mx_ref[rows], s)

    def sum_block(j, diagonal):
        start = pl.multiple_of(j * tk, tk)
        for rows, s, pair in logits(j, diagonal):
            p = jnp.exp(s - mx_ref[rows])
            l_ref[rows] += p
            vb = v_ref[0, pl.ds(start, tk), pair * LANES:(pair + 1) * LANES]
            acc_ref[rows] += jnp.dot(p.astype(BF16), vb, preferred_element_type=F32)

    def loop(fn):
        def pair(t, carry):
            fn(2 * t, False)
            fn(2 * t + 1, False)
            return carry

        def single(j, carry):
            fn(j, False)
            return carry
        lax.fori_loop(0, qi // 2, pair, 0)
        lax.fori_loop((qi // 2) * 2, qi, single, 0)
        fn(qi, True)

    def row_sums():
        smallest = None
        for h in range(FOX_HEADS):
            rows = slice(h * tq, (h + 1) * tq)
            ls = jnp.sum(l_ref[rows], axis=-1, keepdims=True)
            lsum_ref[rows] = ls
            smallest = ls if smallest is None else jnp.minimum(smallest, ls)
        return jnp.min(smallest)

    loop(sum_block)

    @pl.when(row_sums() < UNDERFLOW_GUARD)
    def _():
        mx_ref[...] = jnp.full_like(mx_ref, NEG_BIG)
        l_ref[...] = jnp.zeros_like(l_ref)
        acc_ref[...] = jnp.zeros_like(acc_ref)
        loop(max_block)
        for h in range(FOX_HEADS):
            rows = slice(h * tq, (h + 1) * tq)
            mx_ref[rows] = jnp.broadcast_to(jnp.max(mx_ref[rows], axis=-1, keepdims=True), (tq, tk))
        loop(sum_block)
        row_sums()

    for pair in range(n_pairs):
        o = [acc_ref[h * tq:(h + 1) * tq] / lsum_ref[h * tq:(h + 1) * tq] for h in (2 * pair, 2 * pair + 1)]
        o_ref[0, :, pair * LANES:(pair + 1) * LANES] = jnp.where((lane // FOX_DH) == 0, o[0], o[1]).astype(BF16)


def _fox(big, f_cum, q_col, k_col, v_col):
    b, s, _ = big.shape
    assert s % FOX_TQ == 0, s
    hf = FOX_HEADS * FOX_DH
    nkb = s // FOX_TQ
    f4 = f_cum.reshape(b, FOX_HEADS, nkb, FOX_TQ)
    rows = FOX_HEADS * FOX_TQ
    return pl.pallas_call(
        _fox_kernel,
        grid=(b, s // FOX_TQ),
        in_specs=[
            pl.BlockSpec((1, FOX_TQ, hf), lambda bi, qi: (bi, qi, q_col)),
            pl.BlockSpec((1, s, hf), lambda bi, qi: (bi, 0, k_col)),
            pl.BlockSpec((1, s, hf), lambda bi, qi: (bi, 0, v_col)),
            pl.BlockSpec((1, FOX_HEADS, nkb, FOX_TQ), lambda bi, qi: (bi, 0, 0, 0)),
            pl.BlockSpec((1, FOX_TQ, FOX_HEADS), lambda bi, qi: (bi, qi, 0)),
        ],
        out_specs=pl.BlockSpec((1, FOX_TQ, hf), lambda bi, qi: (bi, qi, 0)),
        out_shape=jax.ShapeDtypeStruct((b, s, hf), BF16),
        scratch_shapes=[pltpu.VMEM((rows, LANES), BF16),
                        pltpu.VMEM((rows, FOX_TQ), F32),
                        pltpu.VMEM((rows, FOX_TQ), F32),
                        pltpu.VMEM((rows, LANES), F32),
                        pltpu.VMEM((rows, 1), F32),
                        pltpu.VMEM((FOX_HEADS // 2, 8, LANES), F32)],
        compiler_params=_params("parallel", "arbitrary"),
        name="fox",
    )(big, big, big, f4, f_cum.transpose(0, 2, 1))


def _t5_bucket_np(dist):
    max_exact = T5_BUCKETS // 2
    d = np.maximum(dist, 1).astype(np.float32)
    large = max_exact + (np.log(d / np.float32(max_exact)) / np.float32(math.log(T5_MAX_DIST / max_exact))
                         * np.float32(T5_BUCKETS - max_exact)).astype(np.int32)
    large = np.minimum(large, T5_BUCKETS - 1)
    return np.where(dist < max_exact, dist, large).astype(np.int32)


N_BIAS_TILES = -(-(DSA_TK + T5_MAX_DIST - 1) // DSA_TQ)
LOG2E = math.log2(math.e)


def _bias_bucket_tiles():
    r = np.arange(DSA_TQ)[:, None]
    c = np.arange(DSA_TK)[None, :]
    return np.stack([_t5_bucket_np(np.maximum(t * DSA_TQ + r - c, 0)) for t in range(N_BIAS_TILES)])


def _bias_kernel(tab_ref, bkt_ref, o_ref):
    h = pl.program_id(1)
    bkt = bkt_ref[0]
    acc = jnp.zeros(bkt.shape, F32)
    for b in range(T5_BUCKETS):
        acc = jnp.where(bkt == b, tab_ref[b, h], acc)
    o_ref[0, 0] = (acc - tab_ref[T5_BUCKETS - 1, h]) * LOG2E


def _bias_tiles(t5_table):
    bkt = jnp.asarray(_bias_bucket_tiles())
    return pl.pallas_call(
        _bias_kernel,
        grid=(N_BIAS_TILES, DSA_HEADS),
        in_specs=[pl.BlockSpec(memory_space=pltpu.SMEM),
                  pl.BlockSpec((1, DSA_TQ, DSA_TK), lambda t, h: (t, 0, 0))],
        out_specs=pl.BlockSpec((1, 1, DSA_TQ, DSA_TK), lambda t, h: (t, h, 0, 0)),
        out_shape=jax.ShapeDtypeStruct((N_BIAS_TILES, DSA_HEADS, DSA_TQ, DSA_TK), F32),
        compiler_params=_params("parallel", "parallel"),
        name="t5_bias",
    )(t5_table, bkt)


def _dsa_kernel(q_ref, qi_ref, wt_ref, k2_ref, c_ref, wuk_ref, wuv_ref, bias_ref, tab_ref, o_ref,
                keyt_ref, hi_ref, lo_ref, selb_ref, qim_ref, qall_ref, p_ref, mx_ref, l_ref, acc_ref,
                lsum_ref, cmax_ref, *, k_top):
    tq, tk = DSA_TQ, DSA_TK
    sub = 8
    i = pl.program_id(1)
    nsb = (i * tq + tq + tk - 1) // tk
    lane = lax.broadcasted_iota(jnp.int32, (1, LANES), 1)
    qpos = i * tq + lax.broadcasted_iota(jnp.int32, (1, tq), 1)
    krow = lax.broadcasted_iota(jnp.int32, (tk, 1), 0)
    zero = jnp.zeros((), BF16)

    w_s = wt_ref[0] * ((IDX_HEADS ** -0.5) * (IDX_DIM ** -0.5))
    for h in range(IDX_HEADS):
        qp = qi_ref[0, :, (h // 2) * LANES:(h // 2 + 1) * LANES]
        qim_ref[h] = jnp.where((lane // IDX_DIM) == (h % 2), qp, zero)

    def score_block(j):
        start = pl.multiple_of(j * tk, tk)
        kb = k2_ref[0, pl.ds(start, tk), :]
        sc = jnp.zeros((tk, tq), F32)
        for h in range(IDX_HEADS):
            lg = lax.dot_general(kb, qim_ref[h], _CONTRACT_LAST, preferred_element_type=F32)
            sc = sc + w_s[h:h + 1, :] * jnp.maximum(lg, 0.0)
        sc = jnp.where(sc == 0.0, 0.0, sc)
        bits = pltpu.bitcast(sc, jnp.int32)
        key = bits ^ ((bits >> 31) & 0x7FFFFFFF)
        key = jnp.where(start + krow <= qpos, key, KEY_NEG_INF)
        keyt_ref[j] = key
        hi_ref[j] = (key >> 16).astype(jnp.int16)
        lo_ref[j] = ((key & 0xFFFF) - HALF).astype(jnp.int16)

    def score_pair(t, carry):
        score_block(2 * t)
        score_block(2 * t + 1)
        return carry

    def score_last(j, carry):
        score_block(j)
        return carry

    lax.fori_loop(0, nsb // 2, score_pair, 0)
    lax.fori_loop((nsb // 2) * 2, nsb, score_last, 0)
    lowest = jnp.full((tk, tq), -HALF, jnp.int16)
    for pad in range(COUNT_STEP - 1):
        hi_ref[nsb + pad] = lowest
        lo_ref[nsb + pad] = lowest
    keyt_ref[nsb] = jnp.full((tk, tq), INT_MIN, jnp.int32)

    def tree_sum(x, rows):
        parts = [x[r * rows:(r + 1) * rows] for r in range(x.shape[0] // rows)]
        while len(parts) > 1:
            parts = [a + b for a, b in zip(parts[::2], parts[1::2])]
        return parts[0]

    def count(pred):
        def body(j, acc):
            return acc + tree_sum(jnp.where(pred(keyt_ref[j], j), 1.0, 0.0), sub)
        per_sublane = lax.fori_loop(0, nsb, body, jnp.zeros((sub, tq), F32))
        return jnp.sum(per_sublane, axis=0, keepdims=True)

    one16, zero16 = jnp.ones((), jnp.int16), jnp.zeros((), jnp.int16)

    def count16(ref, pred):
        def body(t, acc):
            x = jnp.where(pred(ref[COUNT_STEP * t]), one16, zero16)
            for b in range(1, COUNT_STEP):
                x = x + jnp.where(pred(ref[COUNT_STEP * t + b]), one16, zero16)
            return acc + tree_sum(x, 2 * sub)
        per_row = lax.fori_loop(0, (nsb + COUNT_STEP - 1) // COUNT_STEP, body,
                                jnp.zeros((2 * sub, tq), jnp.int16))
        return jnp.sum(per_row.astype(jnp.int32), axis=0, keepdims=True)

    def search16(ref, want):
        def step(t, u):
            cand_u = u | lax.shift_left(jnp.int32(1), 15 - t)
            cand = (cand_u - HALF).astype(jnp.int16)
            n = count16(ref, lambda v: v >= cand)
            return jnp.where(n >= want, cand_u, u)
        return lax.fori_loop(0, 16, step, jnp.zeros((1, tq), jnp.int32)) - HALF

    t_hi = search16(hi_ref, k_top)
    t_hi16 = t_hi.astype(jnp.int16)

    def above_and_ties(t, acc):
        for j in (2 * t, 2 * t + 1):
            hi = hi_ref[j]
            acc = acc + tree_sum(jnp.where(hi > t_hi16, one16, zero16), 2 * sub)
            lo_ref[j] = jnp.where(hi == t_hi16, lo_ref[j], jnp.asarray(-HALF, jnp.int16))
        return acc

    per_row = lax.fori_loop(0, (nsb + 1) // 2, above_and_ties, jnp.zeros((2 * sub, tq), jnp.int16))
    n_above = jnp.sum(per_row.astype(jnp.int32), axis=0, keepdims=True)
    t_lo = search16(lo_ref, k_top - n_above)
    thr = lax.shift_left(t_hi, 16) | (t_lo + HALF)

    def gt_and_ge(t, accs):
        gt, ge = accs
        for j in (2 * t, 2 * t + 1):
            kk = keyt_ref[j]
            gt = gt + tree_sum(jnp.where(kk > thr, 1.0, 0.0), sub)
            ge = ge + tree_sum(jnp.where(kk >= thr, 1.0, 0.0), sub)
        return gt, ge

    zeros = jnp.zeros((sub, tq), F32)
    gt, ge = lax.fori_loop(0, (nsb + 1) // 2, gt_and_ge, (zeros, zeros))
    n_gt = jnp.sum(gt, axis=0, keepdims=True)
    n_ge = jnp.sum(ge, axis=0, keepdims=True)
    need = k_top - n_gt
    excess = (n_ge > k_top) & (thr > KEY_NEG_INF)
    s_total = k2_ref.shape[1]

    def tie_search():
        def step(t, p):
            cand = p | lax.shift_left(jnp.int32(1), (s_total.bit_length() - 1) - t)
            n = count(lambda kk, j: (kk == thr) & (j * tk + krow <= cand - 1))
            return jnp.where(n < need, cand, p)
        p = lax.fori_loop(0, s_total.bit_length(), step, jnp.zeros((1, tq), jnp.int32))
        return jnp.where(excess, p, s_total)

    any_excess = jnp.max(jnp.where(excess, 1.0, 0.0)) > 0.0
    tie_limit = lax.cond(any_excess, tie_search, lambda: jnp.full((1, tq), s_total, jnp.int32))

    def select_blocks(t, carry):
        for j in (2 * t, 2 * t + 1):
            key = keyt_ref[j]
            kpos = j * tk + krow
            sel = ((key > thr) | ((key == thr) & (kpos <= tie_limit))) & (kpos <= qpos)
            selb_ref[j] = jnp.where(sel, 0.0, NEG_BIG).T
        return carry

    lax.fori_loop(0, (nsb + 1) // 2, select_blocks, 0)

    @pl.when(i == 0)
    def _():
        cf = c_ref[0].astype(F32)
        cmax_ref[...] = jnp.full(cmax_ref.shape, jnp.sqrt(jnp.max(jnp.sum(cf * cf, axis=-1, keepdims=True))))

    c_norm = cmax_ref[0:1, 0:1]
    tab = tab_ref[...]
    bias_max = (jnp.max(tab, axis=0, keepdims=True) - tab[T5_BUCKETS - 1:T5_BUCKETS, :]) * LOG2E
    for h in range(DSA_HEADS):
        rows = slice(h * tq, (h + 1) * tq)
        qp = q_ref[0, :, (h // 2) * LANES:(h // 2 + 1) * LANES]
        ql = jnp.dot(qp, wuk_ref[h], preferred_element_type=F32) * ((DSA_DH ** -0.5) * LOG2E)
        ql = ql.astype(BF16)
        qall_ref[rows, :] = ql
        qf = ql.astype(F32)
        n2 = jnp.dot((qf * qf).astype(BF16), jnp.ones((DSA_LATENT, LANES), BF16), preferred_element_type=F32)
        bound = jnp.sqrt(n2) * (BOUND_SLACK * c_norm) + bias_max[:, h:h + 1]
        mx_ref[rows] = jnp.concatenate([bound] * (tk // LANES), axis=1)
    l_ref[...] = jnp.zeros_like(l_ref)
    acc_ref[...] = jnp.zeros_like(acc_ref)
    n_far = jnp.maximum(((i - N_BIAS_TILES) * tq) // tk + 1, 0)

    def logits(j0, nblk, near):
        rc = tq // 2
        start = pl.multiple_of(j0 * tk, tk)
        cb = c_ref[0, pl.ds(start, nblk * tk), :]
        s_all = lax.dot_general(qall_ref[...], cb, _CONTRACT_LAST, preferred_element_type=F32)
        for h in range(DSA_HEADS):
            for r0 in range(0, tq, rc):
                rows = slice(h * tq + r0, h * tq + r0 + rc)
                tiles = []
                for b in range(nblk):
                    s = s_all[rows, b * tk:(b + 1) * tk] + selb_ref[j0 + b, r0:r0 + rc, :]
                    if near:
                        s = s + bias_ref[(i * tq - (j0 + b) * tk) // tq, h, r0:r0 + rc, :]
                    tiles.append(s)
                yield rows, tiles, cb

    def max_blocks(j0, nblk, near):
        for rows, tiles, _ in logits(j0, nblk, near):
            m = mx_ref[rows]
            for s in tiles:
                m = jnp.maximum(m, s)
            mx_ref[rows] = m

    def sum_blocks(j0, nblk, near):
        for rows, tiles, cb in logits(j0, nblk, near):
            m = mx_ref[rows]
            l = l_ref[rows]
            for b, s in enumerate(tiles):
                p = jnp.exp2(s - m)
                l = l + p
                p_ref[rows, b * tk:(b + 1) * tk] = p.astype(BF16)
            l_ref[rows] = l
        acc_ref[...] += jnp.dot(p_ref[:, :nblk * tk], cb, preferred_element_type=F32)

    def sweep(fn):
        def loop(lo, hi, stride, near):
            def body(t, carry):
                fn(lo + t * stride, stride, near)
                return carry
            lax.fori_loop(0, (hi - lo) // stride, body, 0)
        lo, stride = 0, FAR_STEP
        while stride >= 1:
            hi = lo + ((n_far - lo) // stride) * stride
            loop(lo, hi, stride, False)
            lo, stride = hi, stride // 2
        loop(n_far, nsb, 1, True)

    def row_sums():
        smallest = None
        for h in range(DSA_HEADS):
            rows = slice(h * tq, (h + 1) * tq)
            ls = jnp.sum(l_ref[rows], axis=-1, keepdims=True)
            lsum_ref[rows] = ls
            smallest = ls if smallest is None else jnp.minimum(smallest, ls)
        return jnp.min(smallest)

    sweep(sum_blocks)

    @pl.when(row_sums() < UNDERFLOW_GUARD)
    def _():
        mx_ref[...] = jnp.full_like(mx_ref, NEG_BIG)
        l_ref[...] = jnp.zeros_like(l_ref)
        acc_ref[...] = jnp.zeros_like(acc_ref)
        sweep(max_blocks)
        for h in range(DSA_HEADS):
            rows = slice(h * tq, (h + 1) * tq)
            mx_ref[rows] = jnp.broadcast_to(jnp.max(mx_ref[rows], axis=-1, keepdims=True), (tq, tk))
        sweep(sum_blocks)
        row_sums()

    for pair in range(DSA_HEADS // 2):
        out = jnp.zeros((tq, LANES), F32)
        for h in (2 * pair, 2 * pair + 1):
            rows = slice(h * tq, (h + 1) * tq)
            o_lat = (acc_ref[rows] / lsum_ref[rows]).astype(BF16)
            out = out + jnp.dot(o_lat, wuv_ref[h], preferred_element_type=F32)
        o_ref[0, :, pair * LANES:(pair + 1) * LANES] = out.astype(BF16)


def _dsa(big, w_idx_t, c, w_uk, w_uv, bias_tiles, t5_table, q_col, qi_col, k2_col):
    b, s, _ = big.shape
    assert s % DSA_TK == 0 and DSA_TK % DSA_TQ == 0, s
    k_top = min(TOPK_MAX, s // 4)
    hq = DSA_HEADS * DSA_DH
    hi = IDX_HEADS * IDX_DIM
    mine = (np.arange(LANES)[None, :] // DSA_DH) == (np.arange(DSA_HEADS)[:, None] % 2)
    wuk = jnp.where(mine[:, :, None], jnp.concatenate([w_uk, w_uk], axis=1), 0.0).astype(BF16)
    wuv = jnp.where(mine[:, None, :], jnp.concatenate([w_uv, w_uv], axis=2), 0.0).astype(BF16)
    nq = s // DSA_TQ
    rows = DSA_HEADS * DSA_TQ
    return pl.pallas_call(
        functools.partial(_dsa_kernel, k_top=k_top),
        grid=(b, nq),
        in_specs=[
            pl.BlockSpec((1, DSA_TQ, hq), lambda bi, qi: (bi, qi, q_col)),
            pl.BlockSpec((1, DSA_TQ, hi), lambda bi, qi: (bi, qi, qi_col)),
            pl.BlockSpec((1, IDX_HEADS, DSA_TQ), lambda bi, qi: (bi, 0, qi)),
            pl.BlockSpec((1, s, LANES), lambda bi, qi: (bi, 0, k2_col)),
            pl.BlockSpec((1, s, DSA_LATENT), lambda bi, qi: (bi, 0, 0)),
            _resident(wuk.shape), _resident(wuv.shape), _resident(bias_tiles.shape), _resident(t5_table.shape),
        ],
        out_specs=pl.BlockSpec((1, DSA_TQ, hq), lambda bi, qi: (bi, qi, 0)),
        out_shape=jax.ShapeDtypeStruct((b, s, hq), BF16),
        scratch_shapes=[
            pltpu.VMEM((s // DSA_TK + 1, DSA_TK, DSA_TQ), jnp.int32),
            pltpu.VMEM((s // DSA_TK + COUNT_STEP - 1, DSA_TK, DSA_TQ), jnp.int16),
            pltpu.VMEM((s // DSA_TK + COUNT_STEP - 1, DSA_TK, DSA_TQ), jnp.int16),
            pltpu.VMEM((s // DSA_TK + 1, DSA_TQ, DSA_TK), F32),
            pltpu.VMEM((IDX_HEADS, DSA_TQ, LANES), BF16),
            pltpu.VMEM((rows, DSA_LATENT), BF16),
            pltpu.VMEM((rows, FAR_STEP * DSA_TK), BF16),
            pltpu.VMEM((rows, DSA_TK), F32),
            pltpu.VMEM((rows, DSA_TK), F32),
            pltpu.VMEM((rows, DSA_LATENT), F32),
            pltpu.VMEM((rows, 1), F32),
            pltpu.VMEM((8, LANES), F32),
        ],
        compiler_params=_params("parallel", "arbitrary"),
        name="dsa",
    )(big, big, w_idx_t, big, c, wuk, wuv, bias_tiles, t5_table)


def _even_mixer(x2, b, s, g, w_in, w_gate, b_gate, norm_g, fox_b):
    hk = GLA_HEADS * GLA_DK
    hv = GLA_HEADS * GLA_DV
    hf = FOX_HEADS * FOX_DH
    o = np.cumsum([0, hk, hk, hv, hv, GLA_GATE_RANK, hf, hf, hf, FOX_HEADS])
    col = lambda n: w_in[:, o[n]:o[n + 1]]
    w_big = jnp.concatenate([col(0), col(1), col(2), col(3), col(5), col(6), col(7)], axis=1)
    w_small = jnp.pad(jnp.concatenate([col(4), col(8)], axis=1), ((0, 0), (0, LANES - GLA_GATE_RANK - FOX_HEADS)))
    big, small = _inproj(x2, g, w_big, w_small)
    big = big.reshape(b, s, -1)
    small = small.reshape(b, s, LANES)
    o_gla = _gla(big, small, w_gate, b_gate, norm_g)
    f_logit_t = small[:, :, GLA_GATE_RANK:GLA_GATE_RANK + FOX_HEADS].transpose(0, 2, 1)
    f_cum = _fox_gate(f_logit_t, fox_b)
    fox_col = (2 * hk + 2 * hv) // hf
    o_fox = _fox(big, f_cum, fox_col, fox_col + 1, fox_col + 2)
    return [o_gla.reshape(b * s, hv), o_fox.reshape(b * s, hf)]


def _odd_mixer(x2, b, s, g, w_in, kv_g, w_uk, w_uv, bias_tiles, t5_table):
    hq = DSA_HEADS * DSA_DH
    hi = IDX_HEADS * IDX_DIM
    o = np.cumsum([0, hq, DSA_LATENT, hi, IDX_DIM, IDX_HEADS])
    col = lambda n: w_in[:, o[n]:o[n + 1]]
    w_big = jnp.concatenate([col(0), col(2), col(3), col(3)], axis=1)
    w_small = jnp.pad(col(4), ((0, 0), (0, LANES - IDX_HEADS)))
    big, small, c = _inproj(x2, g, w_big, w_small, col(1), kv_g)
    big = big.reshape(b, s, -1)
    w_idx_t = small.reshape(b, s, LANES)[:, :, :IDX_HEADS].transpose(0, 2, 1)
    o_dsa = _dsa(big, w_idx_t, c.reshape(b, s, DSA_LATENT), w_uk, w_uv, bias_tiles, t5_table,
                 0, hq // hi, (hq + hi) // LANES)
    return [o_dsa.reshape(b * s, hq)]


def kernel(x, norm_g, ffn_w_in, ffn_w_out, even_w_in, gla_w_gate, gla_b_gate, gla_norm_g, fox_b_f, even_w_out,
           odd_w_in, mla_kv_norm_g, mla_w_uk, mla_w_uv, odd_w_out, t5_table, final_norm_g):
    b, s, d = x.shape
    depth = norm_g.shape[0]
    x2 = x.reshape(b * s, d)
    bias_tiles = _bias_tiles(t5_table) if depth > 1 else None
    for layer in range(depth):
        g = norm_g[layer]
        j = layer // 2
        x2 = _ffn(x2, g[0], ffn_w_in[layer, 0], ffn_w_out[layer, 0])
        if layer % 2 == 0:
            heads = _even_mixer(x2, b, s, g[1], even_w_in[j], gla_w_gate[j], gla_b_gate[j], gla_norm_g[j], fox_b_f[j])
            w_proj = even_w_out[j]
        else:
            heads = _odd_mixer(x2, b, s, g[1], odd_w_in[j], mla_kv_norm_g[j], mla_w_uk[j], mla_w_uv[j],
                               bias_tiles, t5_table)
            w_proj = odd_w_out[j]
        last = layer == depth - 1
        x2 = _ffn(x2, g[2], ffn_w_in[layer, 1], ffn_w_out[layer, 1], final_norm_g if last else None,
                  mixer=(heads, w_proj))
    return x2.reshape(b, s, d)
```

```python
import functools
import math

import numpy as np
import jax
import jax.numpy as jnp
from jax import lax
from jax.experimental import pallas as pl
from jax.experimental.pallas import tpu as pltpu

F32 = jnp.float32
BF16 = jnp.bfloat16

EPS = 1e-6
GLA_HEADS = 4
GLA_DK = 64
GLA_DV = 128
GLA_GATE_RANK = 16
GLA_GATE_NORMALIZER = 16.0
GLA_CHUNK = 64
FOX_HEADS = 8
FOX_DH = 64
DSA_HEADS = 16
DSA_DH = 64
DSA_LATENT = 256
IDX_HEADS = 8
IDX_DIM = 64
TOPK_MAX = 256
T5_BUCKETS = 32
T5_MAX_DIST = 128

LANES = 128
VMEM_LIMIT = 56 * 1024 * 1024

ROW_TILE = 512
FFN_ROW_TILE = 1024
FFN_CHUNK = 256
GLA_ROWS = 256
GLA_BATCH = 2
FOX_TQ = 512
DSA_TQ = 128
DSA_TK = 256
FAR_STEP = 4
COUNT_STEP = 4
NEG_BIG = -1e30
UNDERFLOW_GUARD = 2.0 ** -60
BOUND_SLACK = 1.01
INT_MIN = -2 ** 31
KEY_NEG_INF = INT_MIN + 0x7FFFFF
HALF = 1 << 15

_CONTRACT_LAST = (((1,), (1,)), ((), ()))
_CONTRACT_FIRST = (((0,), (0,)), ((), ()))


def _params(*sem):
    return pltpu.CompilerParams(dimension_semantics=sem, vmem_limit_bytes=VMEM_LIMIT)


def _resident(shape):
    nd = len(shape)
    return pl.BlockSpec(shape, lambda *_: (0,) * nd, pipeline_mode=pl.Buffered(1))


def _rms(x, g):
    return x * lax.rsqrt(jnp.mean(x * x, axis=-1, keepdims=True) + EPS) * g


def _log_sigmoid(x):
    return jnp.minimum(x, 0.0) - jnp.log1p(jnp.exp(-jnp.abs(x)))


def _silu(x):
    return x * jax.nn.sigmoid(x)


def _ffn_kernel(*refs, n_chunks, final, n_mix):
    x_ref, g_ref, wi_ref, wo_ref = refs[:4]
    mix_refs = refs[4:4 + n_mix]
    proj_refs = refs[4 + n_mix:4 + 2 * n_mix]
    rest = refs[4 + 2 * n_mix:]
    if final:
        gf_ref, o_ref, h_ref, acc_ref = rest
    else:
        o_ref, h_ref, acc_ref = rest
    x = x_ref[...]
    for m_ref, p_ref in zip(mix_refs, proj_refs):
        x = x + jnp.dot(m_ref[...], p_ref[...], preferred_element_type=F32)
    o_ref[...] = x
    h_ref[...] = _rms(x, g_ref[...]).astype(BF16)
    f = wo_ref.shape[0]
    for j in range(n_chunks):
        lo, hi = j * FFN_CHUNK, (j + 1) * FFN_CHUNK
        h = h_ref[...]
        a = jnp.dot(h, wi_ref[:, lo:hi], preferred_element_type=F32)
        b = jnp.dot(h, wi_ref[:, f + lo:f + hi], preferred_element_type=F32)
        act = (_silu(a) * b).astype(BF16)
        part = jnp.dot(act, wo_ref[lo:hi, :], preferred_element_type=F32)
        if j == 0:
            acc_ref[...] = part
        else:
            acc_ref[...] += part
    y = o_ref[...] + 0.5 * acc_ref[...]
    if final:
        y = _rms(y, gf_ref[...])
    o_ref[...] = y


def _ffn(x2, g, w_in, w_out, final_g=None, mixer=None):
    t, d = x2.shape
    f = w_out.shape[0]
    assert t % FFN_ROW_TILE == 0 and f % FFN_CHUNK == 0, (t, f)
    n_chunks = f // FFN_CHUNK
    wi = w_in.astype(BF16)
    wo = w_out.astype(BF16)
    final = final_g is not None
    row = lambda n: pl.BlockSpec((FFN_ROW_TILE, n), lambda i: (i, 0))
    in_specs = [row(d), _resident((1, d)), _resident(wi.shape), _resident(wo.shape)]
    args = [x2, g.reshape(1, d), wi, wo]
    outs, w_proj = mixer if mixer is not None else ((), None)
    lo = 0
    projs = []
    for o in outs:
        projs.append(w_proj[lo:lo + o.shape[1]].astype(BF16))
        lo += o.shape[1]
    in_specs += [row(o.shape[1]) for o in outs] + [_resident(p.shape) for p in projs]
    args += list(outs) + projs
    if final:
        in_specs.append(_resident((1, d)))
        args.append(final_g.reshape(1, d))
    return pl.pallas_call(
        functools.partial(_ffn_kernel, n_chunks=n_chunks, final=final, n_mix=len(outs)),
        grid=(t // FFN_ROW_TILE,),
        in_specs=in_specs,
        out_specs=row(d),
        out_shape=jax.ShapeDtypeStruct((t, d), F32),
        scratch_shapes=[pltpu.VMEM((FFN_ROW_TILE, d), BF16), pltpu.VMEM((FFN_ROW_TILE, d), F32)],
        compiler_params=_params("parallel"),
        name="ffn",
    )(*args)


def _inproj_kernel(*refs, big_chunks, with_c):
    if with_c:
        x_ref, g_ref, wbig_ref, wsm_ref, wc_ref, gc_ref, big_ref, sm_ref, c_ref = refs
    else:
        x_ref, g_ref, wbig_ref, wsm_ref, big_ref, sm_ref = refs
    h = _rms(x_ref[...], g_ref[...]).astype(BF16)
    for lo, hi in big_chunks:
        big_ref[:, lo:hi] = jnp.dot(h, wbig_ref[:, lo:hi], preferred_element_type=F32).astype(BF16)
    sm_ref[...] = jnp.dot(h, wsm_ref[...], preferred_element_type=F32)
    if with_c:
        ckv = jnp.dot(h, wc_ref[...], preferred_element_type=F32)
        c_ref[...] = _rms(ckv, gc_ref[...]).astype(BF16)


def _inproj(x2, g, w_big, w_small, w_c=None, g_c=None):
    t, d = x2.shape
    nbig = w_big.shape[1]
    assert t % ROW_TILE == 0 and nbig % LANES == 0, (t, nbig)
    with_c = w_c is not None
    step = 4 * LANES
    big_chunks = tuple((lo, min(lo + step, nbig)) for lo in range(0, nbig, step))
    row = lambda n: pl.BlockSpec((ROW_TILE, n), lambda i: (i, 0))
    in_specs = [row(d), _resident((1, d)), _resident(w_big.shape), _resident(w_small.shape)]
    args = [x2, g.reshape(1, d), w_big.astype(BF16), w_small.astype(BF16)]
    out_specs = [row(nbig), row(LANES)]
    out_shape = [jax.ShapeDtypeStruct((t, nbig), BF16), jax.ShapeDtypeStruct((t, LANES), F32)]
    if with_c:
        nc = w_c.shape[1]
        in_specs += [_resident(w_c.shape), _resident((1, nc))]
        args += [w_c.astype(BF16), g_c.reshape(1, nc)]
        out_specs.append(row(nc))
        out_shape.append(jax.ShapeDtypeStruct((t, nc), BF16))
    return pl.pallas_call(
        functools.partial(_inproj_kernel, big_chunks=big_chunks, with_c=with_c),
        grid=(t // ROW_TILE,),
        in_specs=in_specs,
        out_specs=out_specs,
        out_shape=out_shape,
        compiler_params=_params("parallel"),
        name="inproj_c" if with_c else "inproj",
    )(*args)


def _gla_kernel(q_ref, k_ref, v_ref, go_ref, sm_ref, wg_ref, bg_ref, ng_ref, o_ref, st_ref):
    nb, rs = q_ref.shape[0], q_ref.shape[1]
    c_len = GLA_CHUNK

    @pl.when(pl.program_id(1) == 0)
    def _():
        st_ref[...] = jnp.zeros_like(st_ref)

    r = lax.broadcasted_iota(jnp.int32, (rs, rs), 0)
    c = lax.broadcasted_iota(jnp.int32, (rs, rs), 1)
    tri = jnp.where(((r // c_len) == (c // c_len)) & (c <= r), 1.0, 0.0).astype(BF16)
    q_dec, k_dec, k_end, decay = [], [], [], []
    for bb in range(nb):
        pre = jnp.dot(sm_ref[bb].astype(BF16), wg_ref[...], preferred_element_type=F32) + bg_ref[...]
        gk = _log_sigmoid(pre) / GLA_GATE_NORMALIZER
        g_cum = jnp.zeros_like(gk)
        rest = gk
        for _ in range(3):
            piece = rest.astype(BF16)
            g_cum = g_cum + jnp.dot(tri, piece, preferred_element_type=F32)
            rest = rest - piece.astype(F32)
        g_last = jnp.concatenate(
            [jnp.broadcast_to(g_cum[ch * c_len + c_len - 1:(ch + 1) * c_len, :], (c_len, g_cum.shape[1]))
             for ch in range(rs // c_len)], axis=0)
        q = q_ref[bb].astype(F32) * (GLA_DK ** -0.5)
        k = k_ref[bb].astype(F32)
        q_dec.append((q * jnp.exp(g_cum)).astype(BF16))
        k_dec.append((k * jnp.exp(-g_cum)).astype(BF16))
        k_end.append((k * jnp.exp(g_last - g_cum)).astype(BF16))
        decay.append(jnp.exp(g_last))

    lane = lax.broadcasted_iota(jnp.int32, (1, LANES), 1)
    ri = lax.broadcasted_iota(jnp.int32, (c_len, c_len), 0)
    ci = lax.broadcasted_iota(jnp.int32, (c_len, c_len), 1)
    causal = ci <= ri
    ng = ng_ref[...]
    zero = jnp.zeros((), BF16)
    for ch in range(rs // c_len):
        rows = slice(ch * c_len, (ch + 1) * c_len)
        for h in range(GLA_HEADS):
            pair = slice((h // 2) * LANES, (h // 2 + 1) * LANES)
            mine = (lane // GLA_DK) == (h % 2)
            vcol = slice(h * GLA_DV, (h + 1) * GLA_DV)
            for bb in range(nb):
                qm = jnp.where(mine, q_dec[bb][rows, pair], zero)
                a = lax.dot_general(qm, k_dec[bb][rows, pair], _CONTRACT_LAST, preferred_element_type=F32)
                a = jnp.where(causal, a, 0.0)
                vh = v_ref[bb, rows, vcol]
                st = st_ref[bb, h]
                o = jnp.dot(a.astype(BF16), vh, preferred_element_type=F32)
                o = o + lax.dot_general(qm, st.astype(BF16), _CONTRACT_LAST, preferred_element_type=F32)
                km = jnp.where(mine, k_end[bb][rows, pair], zero)
                upd = lax.dot_general(vh, km, _CONTRACT_FIRST, preferred_element_type=F32)
                st_ref[bb, h] = st * decay[bb][ch * c_len:ch * c_len + 1, pair] + upd
                on = _rms(o, ng)
                gate = go_ref[bb, rows, vcol].astype(F32)
                o_ref[bb, rows, vcol] = (on * _silu(gate)).astype(BF16)


def _gla(big, small, w_gate, b_gate, norm_g):
    b, s, _ = big.shape
    nb = GLA_BATCH if b % GLA_BATCH == 0 else 1
    assert s % GLA_ROWS == 0, s
    hk = GLA_HEADS * GLA_DK
    hv = GLA_HEADS * GLA_DV
    wg = jnp.pad(w_gate, ((0, LANES - GLA_GATE_RANK), (0, 0))).astype(BF16)
    blk = lambda n, cb: pl.BlockSpec((nb, GLA_ROWS, n), lambda bi, si: (bi, si, cb))
    return pl.pallas_call(
        _gla_kernel,
        grid=(b // nb, s // GLA_ROWS),
        in_specs=[blk(hk, 0), blk(hk, 1), blk(hv, 1), blk(hv, 2), blk(LANES, 0),
                  _resident(wg.shape), _resident((1, hk)), _resident((1, GLA_DV))],
        out_specs=blk(hv, 0),
        out_shape=jax.ShapeDtypeStruct((b, s, hv), BF16),
        scratch_shapes=[pltpu.VMEM((nb, GLA_HEADS, GLA_DV, LANES), F32)],
        compiler_params=_params("parallel", "arbitrary"),
        name="gla",
    )(big, big, big, big, small, wg, b_gate.reshape(1, hk), norm_g.reshape(1, GLA_DV))


def _foxgate_kernel(fl_ref, b_ref, f_ref):
    x = _log_sigmoid(fl_ref[0] + b_ref[...])
    s = x.shape[1]
    lane = lax.broadcasted_iota(jnp.int32, x.shape, 1)
    sh = 1
    while sh < s:
        x = x + jnp.where(lane >= sh, pltpu.roll(x, sh, 1), 0.0)
        sh *= 2
    f_ref[0] = x


def _fox_gate(f_logit_t, bias):
    b, h, s = f_logit_t.shape
    blk = pl.BlockSpec((1, h, s), lambda bi: (bi, 0, 0))
    return pl.pallas_call(
        _foxgate_kernel,
        grid=(b,),
        in_specs=[blk, _resident((h, 1))],
        out_specs=blk,
        out_shape=jax.ShapeDtypeStruct((b, h, s), F32),
        compiler_params=_params("parallel"),
        name="fox_gate",
    )(f_logit_t, bias.reshape(h, 1))


def _fox_kernel(q_ref, k_ref, v_ref, f_ref, fcol_ref, o_ref, qs_ref, mx_ref, l_ref, acc_ref, lsum_ref, kmax_ref):
    tq = q_ref.shape[1]
    tk = tq
    qi = pl.program_id(1)
    n_pairs = FOX_HEADS // 2
    lane = lax.broadcasted_iota(jnp.int32, (1, LANES), 1)
    zero = jnp.zeros((), BF16)

    @pl.when(qi == 0)
    def _():
        for pair in range(n_pairs):
            kf = k_ref[0, :, pair * LANES:(pair + 1) * LANES].astype(F32)
            kmax_ref[pair] = jnp.full(kmax_ref.shape[1:], jnp.sqrt(jnp.max(jnp.sum(kf * kf, axis=-1, keepdims=True))))

    f_rows = fcol_ref[0]
    for h in range(FOX_HEADS):
        rows = slice(h * tq, (h + 1) * tq)
        q = q_ref[0, :, (h // 2) * LANES:(h // 2 + 1) * LANES] * jnp.asarray(FOX_DH ** -0.5, BF16)
        q = jnp.where((lane // FOX_DH) == (h % 2), q, zero)
        qs_ref[rows, :] = q
        qf = q.astype(F32)
        n2 = jnp.dot((qf * qf).astype(BF16), jnp.ones((LANES, LANES), BF16), preferred_element_type=F32)
        bound = jnp.sqrt(n2) * (BOUND_SLACK * kmax_ref[h // 2, 0:1, 0:1]) - f_rows[:, h:h + 1]
        mx_ref[rows] = jnp.concatenate([bound] * (tk // LANES), axis=1)
    l_ref[...] = jnp.zeros_like(l_ref)
    acc_ref[...] = jnp.zeros_like(acc_ref)
    row = lax.broadcasted_iota(jnp.int32, (tq, tk), 0)
    col = lax.broadcasted_iota(jnp.int32, (tq, tk), 1)

    def logits(j, diagonal):
        start = pl.multiple_of(j * tk, tk)
        for pair in range(n_pairs):
            kb = k_ref[0, pl.ds(start, tk), pair * LANES:(pair + 1) * LANES]
            s_all = lax.dot_general(qs_ref[2 * pair * tq:2 * (pair + 1) * tq, :], kb, _CONTRACT_LAST,
                                    preferred_element_type=F32)
            for hh in range(2):
                h = 2 * pair + hh
                s = s_all[hh * tq:(hh + 1) * tq] - f_ref[0, h, pl.ds(j, 1), :]
                if diagonal:
                    s = jnp.where(col <= row, s, NEG_BIG)
                yield slice(h * tq, (h + 1) * tq), s, pair

    def max_block(j, diagonal):
        for rows, s, _ in logits(j, diagonal):
            mx_ref[rows] = jnp.maximum(mx_ref[rows], s)

    def sum_block(j, diagonal):
        start = pl.multiple_of(j * tk, tk)
        for rows, s, pair in logits(j, diagonal):
            p = jnp.exp(s - mx_ref[rows])
            l_ref[rows] += p
            vb = v_ref[0, pl.ds(start, tk), pair * LANES:(pair + 1) * LANES]
            acc_ref[rows] += jnp.dot(p.astype(BF16), vb, preferred_element_type=F32)

    def loop(fn):
        def pair(t, carry):
            fn(2 * t, False)
            fn(2 * t + 1, False)
            return carry

        def single(j, carry):
            fn(j, False)
            return carry
        lax.fori_loop(0, qi // 2, pair, 0)
        lax.fori_loop((qi // 2) * 2, qi, single, 0)
        fn(qi, True)

    def row_sums():
        smallest = None
        for h in range(FOX_HEADS):
            rows = slice(h * tq, (h + 1) * tq)
            ls = jnp.sum(l_ref[rows], axis=-1, keepdims=True)
            lsum_ref[rows] = ls
            smallest = ls if smallest is None else jnp.minimum(smallest, ls)
        return jnp.min(smallest)

    loop(sum_block)

    @pl.when(row_sums() < UNDERFLOW_GUARD)
    def _():
        mx_ref[...] = jnp.full_like(mx_ref, NEG_BIG)
        l_ref[...] = jnp.zeros_like(l_ref)
        acc_ref[...] = jnp.zeros_like(acc_ref)
        loop(max_block)
        for h in range(FOX_HEADS):
            rows = slice(h * tq, (h + 1) * tq)
            mx_ref[rows] = jnp.broadcast_to(jnp.max(mx_ref[rows], axis=-1, keepdims=True), (tq, tk))
        loop(sum_block)
        row_sums()

    for pair in range(n_pairs):
        o = [acc_ref[h * tq:(h + 1) * tq] / lsum_ref[h * tq:(h + 1) * tq] for h in (2 * pair, 2 * pair + 1)]
        o_ref[0, :, pair * LANES:(pair + 1) * LANES] = jnp.where((lane // FOX_DH) == 0, o[0], o[1]).astype(BF16)


def _fox(big, f_cum, q_col, k_col, v_col):
    b, s, _ = big.shape
    assert s % FOX_TQ == 0, s
    hf = FOX_HEADS * FOX_DH
    nkb = s // FOX_TQ
    f4 = f_cum.reshape(b, FOX_HEADS, nkb, FOX_TQ)
    rows = FOX_HEADS * FOX_TQ
    return pl.pallas_call(
        _fox_kernel,
        grid=(b, s // FOX_TQ),
        in_specs=[
            pl.BlockSpec((1, FOX_TQ, hf), lambda bi, qi: (bi, qi, q_col)),
            pl.BlockSpec((1, s, hf), lambda bi, qi: (bi, 0, k_col)),
            pl.BlockSpec((1, s, hf), lambda bi, qi: (bi, 0, v_col)),
            pl.BlockSpec((1, FOX_HEADS, nkb, FOX_TQ), lambda bi, qi: (bi, 0, 0, 0)),
            pl.BlockSpec((1, FOX_TQ, FOX_HEADS), lambda bi, qi: (bi, qi, 0)),
        ],
        out_specs=pl.BlockSpec((1, FOX_TQ, hf), lambda bi, qi: (bi, qi, 0)),
        out_shape=jax.ShapeDtypeStruct((b, s, hf), BF16),
        scratch_shapes=[pltpu.VMEM((rows, LANES), BF16),
                        pltpu.VMEM((rows, FOX_TQ), F32),
                        pltpu.VMEM((rows, FOX_TQ), F32),
                        pltpu.VMEM((rows, LANES), F32),
                        pltpu.VMEM((rows, 1), F32),
                        pltpu.VMEM((FOX_HEADS // 2, 8, LANES), F32)],
        compiler_params=_params("parallel", "arbitrary"),
        name="fox",
    )(big, big, big, f4, f_cum.transpose(0, 2, 1))


def _t5_bucket_np(dist):
    max_exact = T5_BUCKETS // 2
    d = np.maximum(dist, 1).astype(np.float32)
    large = max_exact + (np.log(d / np.float32(max_exact)) / np.float32(math.log(T5_MAX_DIST / max_exact))
                         * np.float32(T5_BUCKETS - max_exact)).astype(np.int32)
    large = np.minimum(large, T5_BUCKETS - 1)
    return np.where(dist < max_exact, dist, large).astype(np.int32)


N_BIAS_TILES = -(-(DSA_TK + T5_MAX_DIST - 1) // DSA_TQ)
LOG2E = math.log2(math.e)


def _bias_bucket_tiles():
    r = np.arange(DSA_TQ)[:, None]
    c = np.arange(DSA_TK)[None, :]
    return np.stack([_t5_bucket_np(np.maximum(t * DSA_TQ + r - c, 0)) for t in range(N_BIAS_TILES)])


def _bias_kernel(tab_ref, bkt_ref, o_ref):
    h = pl.program_id(1)
    bkt = bkt_ref[0]
    acc = jnp.zeros(bkt.shape, F32)
    for b in range(T5_BUCKETS):
        acc = jnp.where(bkt == b, tab_ref[b, h], acc)
    o_ref[0, 0] = (acc - tab_ref[T5_BUCKETS - 1, h]) * LOG2E


def _bias_tiles(t5_table):
    bkt = jnp.asarray(_bias_bucket_tiles())
    return pl.pallas_call(
        _bias_kernel,
        grid=(N_BIAS_TILES, DSA_HEADS),
        in_specs=[pl.BlockSpec(memory_space=pltpu.SMEM),
                  pl.BlockSpec((1, DSA_TQ, DSA_TK), lambda t, h: (t, 0, 0))],
        out_specs=pl.BlockSpec((1, 1, DSA_TQ, DSA_TK), lambda t, h: (t, h, 0, 0)),
        out_shape=jax.ShapeDtypeStruct((N_BIAS_TILES, DSA_HEADS, DSA_TQ, DSA_TK), F32),
        compiler_params=_params("parallel", "parallel"),
        name="t5_bias",
    )(t5_table, bkt)


def _dsa_kernel(q_ref, qi_ref, wt_ref, k2_ref, c_ref, wuk_ref, wuv_ref, bias_ref, tab_ref, o_ref,
                keyt_ref, hi_ref, lo_ref, selb_ref, qim_ref, qall_ref, p_ref, mx_ref, l_ref, acc_ref,
                lsum_ref, cmax_ref, *, k_top):
    tq, tk = DSA_TQ, DSA_TK
    sub = 8
    i = pl.program_id(1)
    nsb = (i * tq + tq + tk - 1) // tk
    lane = lax.broadcasted_iota(jnp.int32, (1, LANES), 1)
    qpos = i * tq + lax.broadcasted_iota(jnp.int32, (1, tq), 1)
    krow = lax.broadcasted_iota(jnp.int32, (tk, 1), 0)
    zero = jnp.zeros((), BF16)

    w_s = wt_ref[0] * ((IDX_HEADS ** -0.5) * (IDX_DIM ** -0.5))
    for h in range(IDX_HEADS):
        qp = qi_ref[0, :, (h // 2) * LANES:(h // 2 + 1) * LANES]
        qm = jnp.where((lane // IDX_DIM) == (h % 2), qp, zero)
        qim_ref[h] = qm.astype(F32).T.astype(BF16)

    def score_block(j):
        start = pl.multiple_of(j * tk, tk)
        kb = k2_ref[0, pl.ds(start, tk), :]
        sc = jnp.zeros((tk, tq), F32)
        for h in range(IDX_HEADS):
            lg = jnp.dot(kb, qim_ref[h], preferred_element_type=F32)
            sc = sc + w_s[h:h + 1, :] * jnp.maximum(lg, 0.0)
        sc = jnp.where(sc == 0.0, 0.0, sc)
        bits = pltpu.bitcast(sc, jnp.int32)
        key = bits ^ ((bits >> 31) & 0x7FFFFFFF)
        key = jnp.where(start + krow <= qpos, key, KEY_NEG_INF)
        keyt_ref[j] = key
        hi_ref[j] = (key >> 16).astype(jnp.int16)
        lo_ref[j] = ((key & 0xFFFF) - HALF).astype(jnp.int16)

    def score_pair(t, carry):
        score_block(2 * t)
        score_block(2 * t + 1)
        return carry

    def score_last(j, carry):
        score_block(j)
        return carry

    lax.fori_loop(0, nsb // 2, score_pair, 0)
    lax.fori_loop((nsb // 2) * 2, nsb, score_last, 0)
    lowest = jnp.full((tk, tq), -HALF, jnp.int16)
    for pad in range(COUNT_STEP - 1):
        hi_ref[nsb + pad] = lowest
        lo_ref[nsb + pad] = lowest
    keyt_ref[nsb] = jnp.full((tk, tq), INT_MIN, jnp.int32)

    def tree_sum(x, rows):
        parts = [x[r * rows:(r + 1) * rows] for r in range(x.shape[0] // rows)]
        while len(parts) > 1:
            parts = [a + b for a, b in zip(parts[::2], parts[1::2])]
        return parts[0]

    def count(pred):
        def body(j, acc):
            return acc + tree_sum(jnp.where(pred(keyt_ref[j], j), 1.0, 0.0), sub)
        per_sublane = lax.fori_loop(0, nsb, body, jnp.zeros((sub, tq), F32))
        return jnp.sum(per_sublane, axis=0, keepdims=True)

    one16, zero16 = jnp.ones((), jnp.int16), jnp.zeros((), jnp.int16)

    def count16(ref, pred):
        def body(t, acc):
            x = jnp.where(pred(ref[COUNT_STEP * t]), one16, zero16)
            for b in range(1, COUNT_STEP):
                x = x + jnp.where(pred(ref[COUNT_STEP * t + b]), one16, zero16)
            return acc + tree_sum(x, 2 * sub)
        per_row = lax.fori_loop(0, (nsb + COUNT_STEP - 1) // COUNT_STEP, body,
                                jnp.zeros((2 * sub, tq), jnp.int16))
        return jnp.sum(per_row.astype(jnp.int32), axis=0, keepdims=True)

    def search16(ref, want):
        def step(t, u):
            cand_u = u | lax.shift_left(jnp.int32(1), 15 - t)
            cand = (cand_u - HALF).astype(jnp.int16)
            n = count16(ref, lambda v: v >= cand)
            return jnp.where(n >= want, cand_u, u)
        return lax.fori_loop(0, 16, step, jnp.zeros((1, tq), jnp.int32)) - HALF

    t_hi = search16(hi_ref, k_top)
    t_hi16 = t_hi.astype(jnp.int16)

    def above_and_ties(t, acc):
        for j in (2 * t, 2 * t + 1):
            hi = hi_ref[j]
            acc = acc + tree_sum(jnp.where(hi > t_hi16, one16, zero16), 2 * sub)
            lo_ref[j] = jnp.where(hi == t_hi16, lo_ref[j], jnp.asarray(-HALF, jnp.int16))
        return acc

    per_row = lax.fori_loop(0, (nsb + 1) // 2, above_and_ties, jnp.zeros((2 * sub, tq), jnp.int16))
    n_above = jnp.sum(per_row.astype(jnp.int32), axis=0, keepdims=True)
    t_lo = search16(lo_ref, k_top - n_above)
    thr = lax.shift_left(t_hi, 16) | (t_lo + HALF)

    def gt_and_ge(t, accs):
        gt, ge = accs
        for j in (2 * t, 2 * t + 1):
            kk = keyt_ref[j]
            gt = gt + tree_sum(jnp.where(kk > thr, 1.0, 0.0), sub)
            ge = ge + tree_sum(jnp.where(kk >= thr, 1.0, 0.0), sub)
        return gt, ge

    zeros = jnp.zeros((sub, tq), F32)
    gt, ge = lax.fori_loop(0, (nsb + 1) // 2, gt_and_ge, (zeros, zeros))
    n_gt = jnp.sum(gt, axis=0, keepdims=True)
    n_ge = jnp.sum(ge, axis=0, keepdims=True)
    need = k_top - n_gt
    excess = (n_ge > k_top) & (thr > KEY_NEG_INF)
    s_total = k2_ref.shape[1]

    def tie_search():
        def step(t, p):
            cand = p | lax.shift_left(jnp.int32(1), (s_total.bit_length() - 1) - t)
            n = count(lambda kk, j: (kk == thr) & (j * tk + krow <= cand - 1))
            return jnp.where(n < need, cand, p)
        p = lax.fori_loop(0, s_total.bit_length(), step, jnp.zeros((1, tq), jnp.int32))
        return jnp.where(excess, p, s_total)

    any_excess = jnp.max(jnp.where(excess, 1.0, 0.0)) > 0.0
    tie_limit = lax.cond(any_excess, tie_search, lambda: jnp.full((1, tq), s_total, jnp.int32))

    def select_blocks(t, carry):
        for j in (2 * t, 2 * t + 1):
            key = keyt_ref[j]
            kpos = j * tk + krow
            sel = ((key > thr) | ((key == thr) & (kpos <= tie_limit))) & (kpos <= qpos)
            selb_ref[j] = jnp.where(sel, 0.0, NEG_BIG).T
        return carry

    lax.fori_loop(0, (nsb + 1) // 2, select_blocks, 0)

    @pl.when(i == 0)
    def _():
        cf = c_ref[0].astype(F32)
        cmax_ref[...] = jnp.full(cmax_ref.shape, jnp.sqrt(jnp.max(jnp.sum(cf * cf, axis=-1, keepdims=True))))

    c_norm = cmax_ref[0:1, 0:1]
    tab = tab_ref[...]
    bias_max = (jnp.max(tab, axis=0, keepdims=True) - tab[T5_BUCKETS - 1:T5_BUCKETS, :]) * LOG2E
    for h in range(DSA_HEADS):
        rows = slice(h * tq, (h + 1) * tq)
        qp = q_ref[0, :, (h // 2) * LANES:(h // 2 + 1) * LANES]
        ql = jnp.dot(qp, wuk_ref[h], preferred_element_type=F32) * ((DSA_DH ** -0.5) * LOG2E)
        ql = ql.astype(BF16)
        qall_ref[rows, :] = ql
        qf = ql.astype(F32)
        n2 = jnp.dot((qf * qf).astype(BF16), jnp.ones((DSA_LATENT, LANES), BF16), preferred_element_type=F32)
        bound = jnp.sqrt(n2) * (BOUND_SLACK * c_norm) + bias_max[:, h:h + 1]
        mx_ref[rows] = jnp.concatenate([bound] * (tk // LANES), axis=1)
    l_ref[...] = jnp.zeros_like(l_ref)
    acc_ref[...] = jnp.zeros_like(acc_ref)
    n_far = jnp.maximum(((i - N_BIAS_TILES) * tq) // tk + 1, 0)

    def logits(j0, nblk, near):
        rc = tq // 2
        start = pl.multiple_of(j0 * tk, tk)
        cb = c_ref[0, pl.ds(start, nblk * tk), :]
        s_all = lax.dot_general(qall_ref[...], cb, _CONTRACT_LAST, preferred_element_type=F32)
        for h in range(DSA_HEADS):
            for r0 in range(0, tq, rc):
                rows = slice(h * tq + r0, h * tq + r0 + rc)
                tiles = []
                for b in range(nblk):
                    s = s_all[rows, b * tk:(b + 1) * tk] + selb_ref[j0 + b, r0:r0 + rc, :]
                    if near:
                        s = s + bias_ref[(i * tq - (j0 + b) * tk) // tq, h, r0:r0 + rc, :]
                    tiles.append(s)
                yield rows, tiles, cb

    def max_blocks(j0, nblk, near):
        for rows, tiles, _ in logits(j0, nblk, near):
            m = mx_ref[rows]
            for s in tiles:
                m = jnp.maximum(m, s)
            mx_ref[rows] = m

    def sum_blocks(j0, nblk, near):
        for rows, tiles, cb in logits(j0, nblk, near):
            m = mx_ref[rows]
            l = l_ref[rows]
            for b, s in enumerate(tiles):
                p = jnp.exp2(s - m)
                l = l + p
                p_ref[rows, b * tk:(b + 1) * tk] = p.astype(BF16)
            l_ref[rows] = l
        acc_ref[...] += jnp.dot(p_ref[:, :nblk * tk], cb, preferred_element_type=F32)

    def sweep(fn):
        def loop(lo, hi, stride, near):
            def body(t, carry):
                fn(lo + t * stride, stride, near)
                return carry
            lax.fori_loop(0, (hi - lo) // stride, body, 0)
        lo, stride = 0, FAR_STEP
        while stride >= 1:
            hi = lo + ((n_far - lo) // stride) * stride
            loop(lo, hi, stride, False)
            lo, stride = hi, stride // 2
        loop(n_far, nsb, 1, True)

    def row_sums():
        smallest = None
        for h in range(DSA_HEADS):
            rows = slice(h * tq, (h + 1) * tq)
            ls = jnp.sum(l_ref[rows], axis=-1, keepdims=True)
            lsum_ref[rows] = ls
            smallest = ls if smallest is None else jnp.minimum(smallest, ls)
        return jnp.min(smallest)

    sweep(sum_blocks)

    @pl.when(row_sums() < UNDERFLOW_GUARD)
    def _():
        mx_ref[...] = jnp.full_like(mx_ref, NEG_BIG)
        l_ref[...] = jnp.zeros_like(l_ref)
        acc_ref[...] = jnp.zeros_like(acc_ref)
        sweep(max_blocks)
        for h in range(DSA_HEADS):
            rows = slice(h * tq, (h + 1) * tq)
            mx_ref[rows] = jnp.broadcast_to(jnp.max(mx_ref[rows], axis=-1, keepdims=True), (tq, tk))
        sweep(sum_blocks)
        row_sums()

    for pair in range(DSA_HEADS // 2):
        out = jnp.zeros((tq, LANES), F32)
        for h in (2 * pair, 2 * pair + 1):
            rows = slice(h * tq, (h + 1) * tq)
            o_lat = (acc_ref[rows] / lsum_ref[rows]).astype(BF16)
            out = out + jnp.dot(o_lat, wuv_ref[h], preferred_element_type=F32)
        o_ref[0, :, pair * LANES:(pair + 1) * LANES] = out.astype(BF16)


def _dsa(big, w_idx_t, c, w_uk, w_uv, bias_tiles, t5_table, q_col, qi_col, k2_col):
    b, s, _ = big.shape
    assert s % DSA_TK == 0 and DSA_TK % DSA_TQ == 0, s
    k_top = min(TOPK_MAX, s // 4)
    hq = DSA_HEADS * DSA_DH
    hi = IDX_HEADS * IDX_DIM
    mine = (np.arange(LANES)[None, :] // DSA_DH) == (np.arange(DSA_HEADS)[:, None] % 2)
    wuk = jnp.where(mine[:, :, None], jnp.concatenate([w_uk, w_uk], axis=1), 0.0).astype(BF16)
    wuv = jnp.where(mine[:, None, :], jnp.concatenate([w_uv, w_uv], axis=2), 0.0).astype(BF16)
    nq = s // DSA_TQ
    rows = DSA_HEADS * DSA_TQ
    return pl.pallas_call(
        functools.partial(_dsa_kernel, k_top=k_top),
        grid=(b, nq),
        in_specs=[
            pl.BlockSpec((1, DSA_TQ, hq), lambda bi, qi: (bi, qi, q_col)),
            pl.BlockSpec((1, DSA_TQ, hi), lambda bi, qi: (bi, qi, qi_col)),
            pl.BlockSpec((1, IDX_HEADS, DSA_TQ), lambda bi, qi: (bi, 0, qi)),
            pl.BlockSpec((1, s, LANES), lambda bi, qi: (bi, 0, k2_col)),
            pl.BlockSpec((1, s, DSA_LATENT), lambda bi, qi: (bi, 0, 0)),
            _resident(wuk.shape), _resident(wuv.shape), _resident(bias_tiles.shape), _resident(t5_table.shape),
        ],
        out_specs=pl.BlockSpec((1, DSA_TQ, hq), lambda bi, qi: (bi, qi, 0)),
        out_shape=jax.ShapeDtypeStruct((b, s, hq), BF16),
        scratch_shapes=[
            pltpu.VMEM((s // DSA_TK + 1, DSA_TK, DSA_TQ), jnp.int32),
            pltpu.VMEM((s // DSA_TK + COUNT_STEP - 1, DSA_TK, DSA_TQ), jnp.int16),
            pltpu.VMEM((s // DSA_TK + COUNT_STEP - 1, DSA_TK, DSA_TQ), jnp.int16),
            pltpu.VMEM((s // DSA_TK + 1, DSA_TQ, DSA_TK), F32),
            pltpu.VMEM((IDX_HEADS, LANES, DSA_TQ), BF16),
            pltpu.VMEM((rows, DSA_LATENT), BF16),
            pltpu.VMEM((rows, FAR_STEP * DSA_TK), BF16),
            pltpu.VMEM((rows, DSA_TK), F32),
            pltpu.VMEM((rows, DSA_TK), F32),
            pltpu.VMEM((rows, DSA_LATENT), F32),
            pltpu.VMEM((rows, 1), F32),
            pltpu.VMEM((8, LANES), F32),
        ],
        compiler_params=_params("parallel", "arbitrary"),
        name="dsa",
    )(big, big, w_idx_t, big, c, wuk, wuv, bias_tiles, t5_table)


def _even_mixer(x2, b, s, g, w_in, w_gate, b_gate, norm_g, fox_b):
    hk = GLA_HEADS * GLA_DK
    hv = GLA_HEADS * GLA_DV
    hf = FOX_HEADS * FOX_DH
    o = np.cumsum([0, hk, hk, hv, hv, GLA_GATE_RANK, hf, hf, hf, FOX_HEADS])
    col = lambda n: w_in[:, o[n]:o[n + 1]]
    w_big = jnp.concatenate([col(0), col(1), col(2), col(3), col(5), col(6), col(7)], axis=1)
    w_small = jnp.pad(jnp.concatenate([col(4), col(8)], axis=1), ((0, 0), (0, LANES - GLA_GATE_RANK - FOX_HEADS)))
    big, small = _inproj(x2, g, w_big, w_small)
    big = big.reshape(b, s, -1)
    small = small.reshape(b, s, LANES)
    o_gla = _gla(big, small, w_gate, b_gate, norm_g)
    f_logit_t = small[:, :, GLA_GATE_RANK:GLA_GATE_RANK + FOX_HEADS].transpose(0, 2, 1)
    f_cum = _fox_gate(f_logit_t, fox_b)
    fox_col = (2 * hk + 2 * hv) // hf
    o_fox = _fox(big, f_cum, fox_col, fox_col + 1, fox_col + 2)
    return [o_gla.reshape(b * s, hv), o_fox.reshape(b * s, hf)]


def _odd_mixer(x2, b, s, g, w_in, kv_g, w_uk, w_uv, bias_tiles, t5_table):
    hq = DSA_HEADS * DSA_DH
    hi = IDX_HEADS * IDX_DIM
    o = np.cumsum([0, hq, DSA_LATENT, hi, IDX_DIM, IDX_HEADS])
    col = lambda n: w_in[:, o[n]:o[n + 1]]
    w_big = jnp.concatenate([col(0), col(2), col(3), col(3)], axis=1)
    w_small = jnp.pad(col(4), ((0, 0), (0, LANES - IDX_HEADS)))
    big, small, c = _inproj(x2, g, w_big, w_small, col(1), kv_g)
    big = big.reshape(b, s, -1)
    w_idx_t = small.reshape(b, s, LANES)[:, :, :IDX_HEADS].transpose(0, 2, 1)
    o_dsa = _dsa(big, w_idx_t, c.reshape(b, s, DSA_LATENT), w_uk, w_uv, bias_tiles, t5_table,
                 0, hq // hi, (hq + hi) // LANES)
    return [o_dsa.reshape(b * s, hq)]


def kernel(x, norm_g, ffn_w_in, ffn_w_out, even_w_in, gla_w_gate, gla_b_gate, gla_norm_g, fox_b_f, even_w_out,
           odd_w_in, mla_kv_norm_g, mla_w_uk, mla_w_uv, odd_w_out, t5_table, final_norm_g):
    b, s, d = x.shape
    depth = norm_g.shape[0]
    x2 = x.reshape(b * s, d)
    bias_tiles = _bias_tiles(t5_table) if depth > 1 else None
    for layer in range(depth):
        g = norm_g[layer]
        j = layer // 2
        x2 = _ffn(x2, g[0], ffn_w_in[layer, 0], ffn_w_out[layer, 0])
        if layer % 2 == 0:
            heads = _even_mixer(x2, b, s, g[1], even_w_in[j], gla_w_gate[j], gla_b_gate[j], gla_norm_g[j], fox_b_f[j])
            w_proj = even_w_out[j]
        else:
            heads = _odd_mixer(x2, b, s, g[1], odd_w_in[j], mla_kv_norm_g[j], mla_w_uk[j], mla_w_uv[j],
                               bias_tiles, t5_table)
            w_proj = odd_w_out[j]
        last = layer == depth - 1
        x2 = _ffn(x2, g[2], ffn_w_in[layer, 1], ffn_w_out[layer, 1], final_norm_g if last else None,
                  mixer=(heads, w_proj))
    return x2.reshape(b, s, d)
```

```python
import functools
import math

import numpy as np
import jax
import jax.numpy as jnp
from jax import lax
from jax.experimental import pallas as pl
from jax.experimental.pallas import tpu as pltpu

F32 = jnp.float32
BF16 = jnp.bfloat16

EPS = 1e-6
GLA_HEADS = 4
GLA_DK = 64
GLA_DV = 128
GLA_GATE_RANK = 16
GLA_GATE_NORMALIZER = 16.0
GLA_CHUNK = 64
FOX_HEADS = 8
FOX_DH = 64
DSA_HEADS = 16
DSA_DH = 64
DSA_LATENT = 256
IDX_HEADS = 8
IDX_DIM = 64
TOPK_MAX = 256
T5_BUCKETS = 32
T5_MAX_DIST = 128

LANES = 128
VMEM_LIMIT = 56 * 1024 * 1024

ROW_TILE = 512
FFN_ROW_TILE = 1024
FFN_CHUNK = 256
GLA_ROWS = 256
GLA_BATCH = 2
FOX_TQ = 512
DSA_TQ = 128
DSA_TK = 256
FAR_STEP = 4
COUNT_STEP = 4
NEG_BIG = -1e30
UNDERFLOW_GUARD = 2.0 ** -60
BOUND_SLACK = 1.01
INT_MIN = -2 ** 31
KEY_NEG_INF = INT_MIN + 0x7FFFFF

_CONTRACT_LAST = (((1,), (1,)), ((), ()))
_CONTRACT_FIRST = (((0,), (0,)), ((), ()))


def _params(*sem):
    return pltpu.CompilerParams(dimension_semantics=sem, vmem_limit_bytes=VMEM_LIMIT)


def _resident(shape):
    nd = len(shape)
    return pl.BlockSpec(shape, lambda *_: (0,) * nd, pipeline_mode=pl.Buffered(1))


def _rms(x, g):
    return x * lax.rsqrt(jnp.mean(x * x, axis=-1, keepdims=True) + EPS) * g


def _log_sigmoid(x):
    return jnp.minimum(x, 0.0) - jnp.log1p(jnp.exp(-jnp.abs(x)))


def _silu(x):
    return x * jax.nn.sigmoid(x)


def _ffn_kernel(*refs, n_chunks, final, n_mix):
    x_ref, g_ref, wi_ref, wo_ref = refs[:4]
    mix_refs = refs[4:4 + n_mix]
    proj_refs = refs[4 + n_mix:4 + 2 * n_mix]
    rest = refs[4 + 2 * n_mix:]
    if final:
        gf_ref, o_ref, h_ref, acc_ref = rest
    else:
        o_ref, h_ref, acc_ref = rest
    x = x_ref[...]
    for m_ref, p_ref in zip(mix_refs, proj_refs):
        x = x + jnp.dot(m_ref[...], p_ref[...], preferred_element_type=F32)
    o_ref[...] = x
    h_ref[...] = _rms(x, g_ref[...]).astype(BF16)
    f = wo_ref.shape[0]
    for j in range(n_chunks):
        lo, hi = j * FFN_CHUNK, (j + 1) * FFN_CHUNK
        h = h_ref[...]
        a = jnp.dot(h, wi_ref[:, lo:hi], preferred_element_type=F32)
        b = jnp.dot(h, wi_ref[:, f + lo:f + hi], preferred_element_type=F32)
        act = (_silu(a) * b).astype(BF16)
        part = jnp.dot(act, wo_ref[lo:hi, :], preferred_element_type=F32)
        if j == 0:
            acc_ref[...] = part
        else:
            acc_ref[...] += part
    y = o_ref[...] + 0.5 * acc_ref[...]
    if final:
        y = _rms(y, gf_ref[...])
    o_ref[...] = y


def _ffn(x2, g, w_in, w_out, final_g=None, mixer=None):
    t, d = x2.shape
    f = w_out.shape[0]
    assert t % FFN_ROW_TILE == 0 and f % FFN_CHUNK == 0, (t, f)
    n_chunks = f // FFN_CHUNK
    wi = w_in.astype(BF16)
    wo = w_out.astype(BF16)
    final = final_g is not None
    row = lambda n: pl.BlockSpec((FFN_ROW_TILE, n), lambda i: (i, 0))
    in_specs = [row(d), _resident((1, d)), _resident(wi.shape), _resident(wo.shape)]
    args = [x2, g.reshape(1, d), wi, wo]
    outs, w_proj = mixer if mixer is not None else ((), None)
    lo = 0
    projs = []
    for o in outs:
        projs.append(w_proj[lo:lo + o.shape[1]].astype(BF16))
        lo += o.shape[1]
    in_specs += [row(o.shape[1]) for o in outs] + [_resident(p.shape) for p in projs]
    args += list(outs) + projs
    if final:
        in_specs.append(_resident((1, d)))
        args.append(final_g.reshape(1, d))
    return pl.pallas_call(
        functools.partial(_ffn_kernel, n_chunks=n_chunks, final=final, n_mix=len(outs)),
        grid=(t // FFN_ROW_TILE,),
        in_specs=in_specs,
        out_specs=row(d),
        out_shape=jax.ShapeDtypeStruct((t, d), F32),
        scratch_shapes=[pltpu.VMEM((FFN_ROW_TILE, d), BF16), pltpu.VMEM((FFN_ROW_TILE, d), F32)],
        compiler_params=_params("parallel"),
        name="ffn",
    )(*args)


def _inproj_kernel(*refs, big_chunks, with_c):
    if with_c:
        x_ref, g_ref, wbig_ref, wsm_ref, wc_ref, gc_ref, big_ref, sm_ref, c_ref = refs
    else:
        x_ref, g_ref, wbig_ref, wsm_ref, big_ref, sm_ref = refs
    h = _rms(x_ref[...], g_ref[...]).astype(BF16)
    for lo, hi in big_chunks:
        big_ref[:, lo:hi] = jnp.dot(h, wbig_ref[:, lo:hi], preferred_element_type=F32).astype(BF16)
    sm_ref[...] = jnp.dot(h, wsm_ref[...], preferred_element_type=F32)
    if with_c:
        ckv = jnp.dot(h, wc_ref[...], preferred_element_type=F32)
        c_ref[...] = _rms(ckv, gc_ref[...]).astype(BF16)


def _inproj(x2, g, w_big, w_small, w_c=None, g_c=None):
    t, d = x2.shape
    nbig = w_big.shape[1]
    assert t % ROW_TILE == 0 and nbig % LANES == 0, (t, nbig)
    with_c = w_c is not None
    step = 4 * LANES
    big_chunks = tuple((lo, min(lo + step, nbig)) for lo in range(0, nbig, step))
    row = lambda n: pl.BlockSpec((ROW_TILE, n), lambda i: (i, 0))
    in_specs = [row(d), _resident((1, d)), _resident(w_big.shape), _resident(w_small.shape)]
    args = [x2, g.reshape(1, d), w_big.astype(BF16), w_small.astype(BF16)]
    out_specs = [row(nbig), row(LANES)]
    out_shape = [jax.ShapeDtypeStruct((t, nbig), BF16), jax.ShapeDtypeStruct((t, LANES), F32)]
    if with_c:
        nc = w_c.shape[1]
        in_specs += [_resident(w_c.shape), _resident((1, nc))]
        args += [w_c.astype(BF16), g_c.reshape(1, nc)]
        out_specs.append(row(nc))
        out_shape.append(jax.ShapeDtypeStruct((t, nc), BF16))
    return pl.pallas_call(
        functools.partial(_inproj_kernel, big_chunks=big_chunks, with_c=with_c),
        grid=(t // ROW_TILE,),
        in_specs=in_specs,
        out_specs=out_specs,
        out_shape=out_shape,
        compiler_params=_params("parallel"),
        name="inproj_c" if with_c else "inproj",
    )(*args)


def _gla_kernel(q_ref, k_ref, v_ref, go_ref, sm_ref, wg_ref, bg_ref, ng_ref, o_ref, st_ref):
    nb, rs = q_ref.shape[0], q_ref.shape[1]
    c_len = GLA_CHUNK

    @pl.when(pl.program_id(1) == 0)
    def _():
        st_ref[...] = jnp.zeros_like(st_ref)

    r = lax.broadcasted_iota(jnp.int32, (rs, rs), 0)
    c = lax.broadcasted_iota(jnp.int32, (rs, rs), 1)
    tri = jnp.where(((r // c_len) == (c // c_len)) & (c <= r), 1.0, 0.0).astype(BF16)
    q_dec, k_dec, k_end, decay = [], [], [], []
    for bb in range(nb):
        pre = jnp.dot(sm_ref[bb].astype(BF16), wg_ref[...], preferred_element_type=F32) + bg_ref[...]
        gk = _log_sigmoid(pre) / GLA_GATE_NORMALIZER
        g_cum = jnp.zeros_like(gk)
        rest = gk
        for _ in range(3):
            piece = rest.astype(BF16)
            g_cum = g_cum + jnp.dot(tri, piece, preferred_element_type=F32)
            rest = rest - piece.astype(F32)
        g_last = jnp.concatenate(
            [jnp.broadcast_to(g_cum[ch * c_len + c_len - 1:(ch + 1) * c_len, :], (c_len, g_cum.shape[1]))
             for ch in range(rs // c_len)], axis=0)
        q = q_ref[bb].astype(F32) * (GLA_DK ** -0.5)
        k = k_ref[bb].astype(F32)
        q_dec.append((q * jnp.exp(g_cum)).astype(BF16))
        k_dec.append((k * jnp.exp(-g_cum)).astype(BF16))
        k_end.append((k * jnp.exp(g_last - g_cum)).astype(BF16))
        decay.append(jnp.exp(g_last))

    lane = lax.broadcasted_iota(jnp.int32, (1, LANES), 1)
    ri = lax.broadcasted_iota(jnp.int32, (c_len, c_len), 0)
    ci = lax.broadcasted_iota(jnp.int32, (c_len, c_len), 1)
    causal = ci <= ri
    ng = ng_ref[...]
    zero = jnp.zeros((), BF16)
    for ch in range(rs // c_len):
        rows = slice(ch * c_len, (ch + 1) * c_len)
        for h in range(GLA_HEADS):
            pair = slice((h // 2) * LANES, (h // 2 + 1) * LANES)
            mine = (lane // GLA_DK) == (h % 2)
            vcol = slice(h * GLA_DV, (h + 1) * GLA_DV)
            for bb in range(nb):
                qm = jnp.where(mine, q_dec[bb][rows, pair], zero)
                a = lax.dot_general(qm, k_dec[bb][rows, pair], _CONTRACT_LAST, preferred_element_type=F32)
                a = jnp.where(causal, a, 0.0)
                vh = v_ref[bb, rows, vcol]
                st = st_ref[bb, h]
                o = jnp.dot(a.astype(BF16), vh, preferred_element_type=F32)
                o = o + lax.dot_general(qm, st.astype(BF16), _CONTRACT_LAST, preferred_element_type=F32)
                km = jnp.where(mine, k_end[bb][rows, pair], zero)
                upd = lax.dot_general(vh, km, _CONTRACT_FIRST, preferred_element_type=F32)
                st_ref[bb, h] = st * decay[bb][ch * c_len:ch * c_len + 1, pair] + upd
                on = _rms(o, ng)
                gate = go_ref[bb, rows, vcol].astype(F32)
                o_ref[bb, rows, vcol] = (on * _silu(gate)).astype(BF16)


def _gla(big, small, w_gate, b_gate, norm_g):
    b, s, _ = big.shape
    nb = GLA_BATCH if b % GLA_BATCH == 0 else 1
    assert s % GLA_ROWS == 0, s
    hk = GLA_HEADS * GLA_DK
    hv = GLA_HEADS * GLA_DV
    wg = jnp.pad(w_gate, ((0, LANES - GLA_GATE_RANK), (0, 0))).astype(BF16)
    blk = lambda n, cb: pl.BlockSpec((nb, GLA_ROWS, n), lambda bi, si: (bi, si, cb))
    return pl.pallas_call(
        _gla_kernel,
        grid=(b // nb, s // GLA_ROWS),
        in_specs=[blk(hk, 0), blk(hk, 1), blk(hv, 1), blk(hv, 2), blk(LANES, 0),
                  _resident(wg.shape), _resident((1, hk)), _resident((1, GLA_DV))],
        out_specs=blk(hv, 0),
        out_shape=jax.ShapeDtypeStruct((b, s, hv), BF16),
        scratch_shapes=[pltpu.VMEM((nb, GLA_HEADS, GLA_DV, LANES), F32)],
        compiler_params=_params("parallel", "arbitrary"),
        name="gla",
    )(big, big, big, big, small, wg, b_gate.reshape(1, hk), norm_g.reshape(1, GLA_DV))


def _foxgate_kernel(fl_ref, b_ref, f_ref):
    x = _log_sigmoid(fl_ref[0] + b_ref[...])
    s = x.shape[1]
    lane = lax.broadcasted_iota(jnp.int32, x.shape, 1)
    sh = 1
    while sh < s:
        x = x + jnp.where(lane >= sh, pltpu.roll(x, sh, 1), 0.0)
        sh *= 2
    f_ref[0] = x


def _fox_gate(f_logit_t, bias):
    b, h, s = f_logit_t.shape
    blk = pl.BlockSpec((1, h, s), lambda bi: (bi, 0, 0))
    return pl.pallas_call(
        _foxgate_kernel,
        grid=(b,),
        in_specs=[blk, _resident((h, 1))],
        out_specs=blk,
        out_shape=jax.ShapeDtypeStruct((b, h, s), F32),
        compiler_params=_params("parallel"),
        name="fox_gate",
    )(f_logit_t, bias.reshape(h, 1))


def _fox_kernel(q_ref, k_ref, v_ref, f_ref, fcol_ref, o_ref, qs_ref, mx_ref, l_ref, acc_ref, lsum_ref, kmax_ref):
    tq = q_ref.shape[1]
    tk = tq
    qi = pl.program_id(1)
    n_pairs = FOX_HEADS // 2
    lane = lax.broadcasted_iota(jnp.int32, (1, LANES), 1)
    zero = jnp.zeros((), BF16)

    @pl.when(qi == 0)
    def _():
        for pair in range(n_pairs):
            kf = k_ref[0, :, pair * LANES:(pair + 1) * LANES].astype(F32)
            kmax_ref[pair] = jnp.full(kmax_ref.shape[1:], jnp.sqrt(jnp.max(jnp.sum(kf * kf, axis=-1, keepdims=True))))

    f_rows = fcol_ref[0]
    for h in range(FOX_HEADS):
        rows = slice(h * tq, (h + 1) * tq)
        q = q_ref[0, :, (h // 2) * LANES:(h // 2 + 1) * LANES] * jnp.asarray(FOX_DH ** -0.5, BF16)
        q = jnp.where((lane // FOX_DH) == (h % 2), q, zero)
        qs_ref[rows, :] = q
        qf = q.astype(F32)
        n2 = jnp.dot((qf * qf).astype(BF16), jnp.ones((LANES, LANES), BF16), preferred_element_type=F32)
        bound = jnp.sqrt(n2) * (BOUND_SLACK * kmax_ref[h // 2, 0:1, 0:1]) - f_rows[:, h:h + 1]
        mx_ref[rows] = jnp.concatenate([bound] * (tk // LANES), axis=1)
    l_ref[...] = jnp.zeros_like(l_ref)
    acc_ref[...] = jnp.zeros_like(acc_ref)
    row = lax.broadcasted_iota(jnp.int32, (tq, tk), 0)
    col = lax.broadcasted_iota(jnp.int32, (tq, tk), 1)

    def logits(j, diagonal):
        start = pl.multiple_of(j * tk, tk)
        for pair in range(n_pairs):
            kb = k_ref[0, pl.ds(start, tk), pair * LANES:(pair + 1) * LANES]
            s_all = lax.dot_general(qs_ref[2 * pair * tq:2 * (pair + 1) * tq, :], kb, _CONTRACT_LAST,
                                    preferred_element_type=F32)
            for hh in range(2):
                h = 2 * pair + hh
                s = s_all[hh * tq:(hh + 1) * tq] - f_ref[0, h, pl.ds(j, 1), :]
                if diagonal:
                    s = jnp.where(col <= row, s, NEG_BIG)
                yield slice(h * tq, (h + 1) * tq), s, pair

    def max_block(j, diagonal):
        for rows, s, _ in logits(j, diagonal):
            mx_ref[rows] = jnp.maximum(mx_ref[rows], s)

    def sum_block(j, diagonal):
        start = pl.multiple_of(j * tk, tk)
        for rows, s, pair in logits(j, diagonal):
            p = jnp.exp(s - mx_ref[rows])
            l_ref[rows] += p
            vb = v_ref[0, pl.ds(start, tk), pair * LANES:(pair + 1) * LANES]
            acc_ref[rows] += jnp.dot(p.astype(BF16), vb, preferred_element_type=F32)

    def loop(fn):
        def pair(t, carry):
            fn(2 * t, False)
            fn(2 * t + 1, False)
            return carry

        def single(j, carry):
            fn(j, False)
            return carry
        lax.fori_loop(0, qi // 2, pair, 0)
        lax.fori_loop((qi // 2) * 2, qi, single, 0)
        fn(qi, True)

    def row_sums():
        smallest = None
        for h in range(FOX_HEADS):
            rows = slice(h * tq, (h + 1) * tq)
            ls = jnp.sum(l_ref[rows], axis=-1, keepdims=True)
            lsum_ref[rows] = ls
            smallest = ls if smallest is None else jnp.minimum(smallest, ls)
        return jnp.min(smallest)

    loop(sum_block)

    @pl.when(row_sums() < UNDERFLOW_GUARD)
    def _():
        mx_ref[...] = jnp.full_like(mx_ref, NEG_BIG)
        l_ref[...] = jnp.zeros_like(l_ref)
        acc_ref[...] = jnp.zeros_like(acc_ref)
        loop(max_block)
        for h in range(FOX_HEADS):
            rows = slice(h * tq, (h + 1) * tq)
            mx_ref[rows] = jnp.broadcast_to(jnp.max(mx_ref[rows], axis=-1, keepdims=True), (tq, tk))
        loop(sum_block)
        row_sums()

    for pair in range(n_pairs):
        o = [acc_ref[h * tq:(h + 1) * tq] / lsum_ref[h * tq:(h + 1) * tq] for h in (2 * pair, 2 * pair + 1)]
        o_ref[0, :, pair * LANES:(pair + 1) * LANES] = jnp.where((lane // FOX_DH) == 0, o[0], o[1]).astype(BF16)


def _fox(big, f_cum, q_col, k_col, v_col):
    b, s, _ = big.shape
    assert s % FOX_TQ == 0, s
    hf = FOX_HEADS * FOX_DH
    nkb = s // FOX_TQ
    f4 = f_cum.reshape(b, FOX_HEADS, nkb, FOX_TQ)
    rows = FOX_HEADS * FOX_TQ
    return pl.pallas_call(
        _fox_kernel,
        grid=(b, s // FOX_TQ),
        in_specs=[
            pl.BlockSpec((1, FOX_TQ, hf), lambda bi, qi: (bi, qi, q_col)),
            pl.BlockSpec((1, s, hf), lambda bi, qi: (bi, 0, k_col)),
            pl.BlockSpec((1, s, hf), lambda bi, qi: (bi, 0, v_col)),
            pl.BlockSpec((1, FOX_HEADS, nkb, FOX_TQ), lambda bi, qi: (bi, 0, 0, 0)),
            pl.BlockSpec((1, FOX_TQ, FOX_HEADS), lambda bi, qi: (bi, qi, 0)),
        ],
        out_specs=pl.BlockSpec((1, FOX_TQ, hf), lambda bi, qi: (bi, qi, 0)),
        out_shape=jax.ShapeDtypeStruct((b, s, hf), BF16),
        scratch_shapes=[pltpu.VMEM((rows, LANES), BF16),
                        pltpu.VMEM((rows, FOX_TQ), F32),
                        pltpu.VMEM((rows, FOX_TQ), F32),
                        pltpu.VMEM((rows, LANES), F32),
                        pltpu.VMEM((rows, 1), F32),
                        pltpu.VMEM((FOX_HEADS // 2, 8, LANES), F32)],
        compiler_params=_params("parallel", "arbitrary"),
        name="fox",
    )(big, big, big, f4, f_cum.transpose(0, 2, 1))


def _t5_bucket_np(dist):
    max_exact = T5_BUCKETS // 2
    d = np.maximum(dist, 1).astype(np.float32)
    large = max_exact + (np.log(d / np.float32(max_exact)) / np.float32(math.log(T5_MAX_DIST / max_exact))
                         * np.float32(T5_BUCKETS - max_exact)).astype(np.int32)
    large = np.minimum(large, T5_BUCKETS - 1)
    return np.where(dist < max_exact, dist, large).astype(np.int32)


N_BIAS_TILES = -(-(DSA_TK + T5_MAX_DIST - 1) // DSA_TQ)
LOG2E = math.log2(math.e)


def _bias_bucket_tiles():
    r = np.arange(DSA_TQ)[:, None]
    c = np.arange(DSA_TK)[None, :]
    return np.stack([_t5_bucket_np(np.maximum(t * DSA_TQ + r - c, 0)) for t in range(N_BIAS_TILES)])


def _bias_kernel(tab_ref, bkt_ref, o_ref):
    h = pl.program_id(1)
    bkt = bkt_ref[0]
    acc = jnp.zeros(bkt.shape, F32)
    for b in range(T5_BUCKETS):
        acc = jnp.where(bkt == b, tab_ref[b, h], acc)
    o_ref[0, 0] = (acc - tab_ref[T5_BUCKETS - 1, h]) * LOG2E


def _bias_tiles(t5_table):
    bkt = jnp.asarray(_bias_bucket_tiles())
    return pl.pallas_call(
        _bias_kernel,
        grid=(N_BIAS_TILES, DSA_HEADS),
        in_specs=[pl.BlockSpec(memory_space=pltpu.SMEM),
                  pl.BlockSpec((1, DSA_TQ, DSA_TK), lambda t, h: (t, 0, 0))],
        out_specs=pl.BlockSpec((1, 1, DSA_TQ, DSA_TK), lambda t, h: (t, h, 0, 0)),
        out_shape=jax.ShapeDtypeStruct((N_BIAS_TILES, DSA_HEADS, DSA_TQ, DSA_TK), F32),
        compiler_params=_params("parallel", "parallel"),
        name="t5_bias",
    )(t5_table, bkt)


def _dsa_kernel(q_ref, qi_ref, wt_ref, k2_ref, c_ref, wuk_ref, wuv_ref, bias_ref, tab_ref, o_ref,
                keyt_ref, selb_ref, qim_ref, qall_ref, p_ref, mx_ref, l_ref, acc_ref,
                lsum_ref, cmax_ref, *, k_top):
    tq, tk = DSA_TQ, DSA_TK
    sub = 8
    i = pl.program_id(1)
    nsb = (i * tq + tq + tk - 1) // tk
    lane = lax.broadcasted_iota(jnp.int32, (1, LANES), 1)
    qpos = i * tq + lax.broadcasted_iota(jnp.int32, (1, tq), 1)
    krow = lax.broadcasted_iota(jnp.int32, (tk, 1), 0)
    zero = jnp.zeros((), BF16)

    w_s = wt_ref[0] * ((IDX_HEADS ** -0.5) * (IDX_DIM ** -0.5))
    for h in range(IDX_HEADS):
        qp = qi_ref[0, :, (h // 2) * LANES:(h // 2 + 1) * LANES]
        qm = jnp.where((lane // IDX_DIM) == (h % 2), qp, zero)
        qim_ref[h] = qm.astype(F32).T.astype(BF16)

    def score_block(j):
        start = pl.multiple_of(j * tk, tk)
        kb = k2_ref[0, pl.ds(start, tk), :]
        sc = jnp.zeros((tk, tq), F32)
        for h in range(IDX_HEADS):
            lg = jnp.dot(kb, qim_ref[h], preferred_element_type=F32)
            sc = sc + w_s[h:h + 1, :] * jnp.maximum(lg, 0.0)
        sc = jnp.where(sc == 0.0, 0.0, sc)
        bits = pltpu.bitcast(sc, jnp.int32)
        key = bits ^ ((bits >> 31) & 0x7FFFFFFF)
        key = jnp.where(start + krow <= qpos, key, KEY_NEG_INF)
        keyt_ref[j] = key

    def score_pair(t, carry):
        score_block(2 * t)
        score_block(2 * t + 1)
        return carry

    def score_last(j, carry):
        score_block(j)
        return carry

    lax.fori_loop(0, nsb // 2, score_pair, 0)
    lax.fori_loop((nsb // 2) * 2, nsb, score_last, 0)
    for pad in range(COUNT_STEP - 1):
        keyt_ref[nsb + pad] = jnp.full((tk, tq), INT_MIN, jnp.int32)

    def tree_sum(x, rows):
        parts = [x[r * rows:(r + 1) * rows] for r in range(x.shape[0] // rows)]
        while len(parts) > 1:
            parts = [a + b for a, b in zip(parts[::2], parts[1::2])]
        return parts[0]

    def count(pred):
        def body(j, acc):
            return acc + tree_sum(jnp.where(pred(keyt_ref[j], j), 1.0, 0.0), sub)
        per_sublane = lax.fori_loop(0, nsb, body, jnp.zeros((sub, tq), F32))
        return jnp.sum(per_sublane, axis=0, keepdims=True)

    def bit_step(t, u):
        cand_u = u | lax.shift_left(jnp.int32(1), 31 - t)
        cand = cand_u ^ INT_MIN

        def body(s, acc):
            x = jnp.where(keyt_ref[COUNT_STEP * s] >= cand, 1.0, 0.0)
            for b in range(1, COUNT_STEP):
                x = x + jnp.where(keyt_ref[COUNT_STEP * s + b] >= cand, 1.0, 0.0)
            return acc + tree_sum(x, sub)
        per_sublane = lax.fori_loop(0, (nsb + COUNT_STEP - 1) // COUNT_STEP, body, jnp.zeros((sub, tq), F32))
        n = jnp.sum(per_sublane, axis=0, keepdims=True)
        return jnp.where(n >= k_top, cand_u, u)

    thr = lax.fori_loop(0, 32, bit_step, jnp.zeros((1, tq), jnp.int32)) ^ INT_MIN

    def gt_and_ge(t, accs):
        gt, ge = accs
        for j in (2 * t, 2 * t + 1):
            kk = keyt_ref[j]
            gt = gt + tree_sum(jnp.where(kk > thr, 1.0, 0.0), sub)
            ge = ge + tree_sum(jnp.where(kk >= thr, 1.0, 0.0), sub)
        return gt, ge

    zeros = jnp.zeros((sub, tq), F32)
    gt, ge = lax.fori_loop(0, (nsb + 1) // 2, gt_and_ge, (zeros, zeros))
    n_gt = jnp.sum(gt, axis=0, keepdims=True)
    n_ge = jnp.sum(ge, axis=0, keepdims=True)
    need = k_top - n_gt
    excess = (n_ge > k_top) & (thr > KEY_NEG_INF)
    s_total = k2_ref.shape[1]

    def tie_search():
        def step(t, p):
            cand = p | lax.shift_left(jnp.int32(1), (s_total.bit_length() - 1) - t)
            n = count(lambda kk, j: (kk == thr) & (j * tk + krow <= cand - 1))
            return jnp.where(n < need, cand, p)
        p = lax.fori_loop(0, s_total.bit_length(), step, jnp.zeros((1, tq), jnp.int32))
        return jnp.where(excess, p, s_total)

    any_excess = jnp.max(jnp.where(excess, 1.0, 0.0)) > 0.0
    tie_limit = lax.cond(any_excess, tie_search, lambda: jnp.full((1, tq), s_total, jnp.int32))

    def select_blocks(t, carry):
        for j in (2 * t, 2 * t + 1):
            key = keyt_ref[j]
            kpos = j * tk + krow
            sel = ((key > thr) | ((key == thr) & (kpos <= tie_limit))) & (kpos <= qpos)
            selb_ref[j] = jnp.where(sel, 0.0, NEG_BIG).T
        return carry

    lax.fori_loop(0, (nsb + 1) // 2, select_blocks, 0)

    @pl.when(i == 0)
    def _():
        cf = c_ref[0].astype(F32)
        cmax_ref[...] = jnp.full(cmax_ref.shape, jnp.sqrt(jnp.max(jnp.sum(cf * cf, axis=-1, keepdims=True))))

    c_norm = cmax_ref[0:1, 0:1]
    tab = tab_ref[...]
    bias_max = (jnp.max(tab, axis=0, keepdims=True) - tab[T5_BUCKETS - 1:T5_BUCKETS, :]) * LOG2E
    for h in range(DSA_HEADS):
        rows = slice(h * tq, (h + 1) * tq)
        qp = q_ref[0, :, (h // 2) * LANES:(h // 2 + 1) * LANES]
        ql = jnp.dot(qp, wuk_ref[h], preferred_element_type=F32) * ((DSA_DH ** -0.5) * LOG2E)
        ql = ql.astype(BF16)
        qall_ref[rows, :] = ql
        qf = ql.astype(F32)
        n2 = jnp.dot((qf * qf).astype(BF16), jnp.ones((DSA_LATENT, LANES), BF16), preferred_element_type=F32)
        bound = jnp.sqrt(n2) * (BOUND_SLACK * c_norm) + bias_max[:, h:h + 1]
        mx_ref[rows] = jnp.concatenate([bound] * (tk // LANES), axis=1)
    l_ref[...] = jnp.zeros_like(l_ref)
    acc_ref[...] = jnp.zeros_like(acc_ref)
    n_far = jnp.maximum(((i - N_BIAS_TILES) * tq) // tk + 1, 0)

    def logits(j0, nblk, near):
        rc = tq // 2
        start = pl.multiple_of(j0 * tk, tk)
        cb = c_ref[0, pl.ds(start, nblk * tk), :]
        s_all = lax.dot_general(qall_ref[...], cb, _CONTRACT_LAST, preferred_element_type=F32)
        for h in range(DSA_HEADS):
            for r0 in range(0, tq, rc):
                rows = slice(h * tq + r0, h * tq + r0 + rc)
                tiles = []
                for b in range(nblk):
                    s = s_all[rows, b * tk:(b + 1) * tk] + selb_ref[j0 + b, r0:r0 + rc, :]
                    if near:
                        s = s + bias_ref[(i * tq - (j0 + b) * tk) // tq, h, r0:r0 + rc, :]
                    tiles.append(s)
                yield rows, tiles, cb

    def max_blocks(j0, nblk, near):
        for rows, tiles, _ in logits(j0, nblk, near):
            m = mx_ref[rows]
            for s in tiles:
                m = jnp.maximum(m, s)
            mx_ref[rows] = m

    def sum_blocks(j0, nblk, near):
        for rows, tiles, cb in logits(j0, nblk, near):
            m = mx_ref[rows]
            l = l_ref[rows]
            for b, s in enumerate(tiles):
                p = jnp.exp2(s - m)
                l = l + p
                p_ref[rows, b * tk:(b + 1) * tk] = p.astype(BF16)
            l_ref[rows] = l
        acc_ref[...] += jnp.dot(p_ref[:, :nblk * tk], cb, preferred_element_type=F32)

    def sweep(fn):
        def loop(lo, hi, stride, near):
            def body(t, carry):
                fn(lo + t * stride, stride, near)
                return carry
            lax.fori_loop(0, (hi - lo) // stride, body, 0)
        lo, stride = 0, FAR_STEP
        while stride >= 1:
            hi = lo + ((n_far - lo) // stride) * stride
            loop(lo, hi, stride, False)
            lo, stride = hi, stride // 2
        loop(n_far, nsb, 1, True)

    def row_sums():
        smallest = None
        for h in range(DSA_HEADS):
            rows = slice(h * tq, (h + 1) * tq)
            ls = jnp.sum(l_ref[rows], axis=-1, keepdims=True)
            lsum_ref[rows] = ls
            smallest = ls if smallest is None else jnp.minimum(smallest, ls)
        return jnp.min(smallest)

    sweep(sum_blocks)

    @pl.when(row_sums() < UNDERFLOW_GUARD)
    def _():
        mx_ref[...] = jnp.full_like(mx_ref, NEG_BIG)
        l_ref[...] = jnp.zeros_like(l_ref)
        acc_ref[...] = jnp.zeros_like(acc_ref)
        sweep(max_blocks)
        for h in range(DSA_HEADS):
            rows = slice(h * tq, (h + 1) * tq)
            mx_ref[rows] = jnp.broadcast_to(jnp.max(mx_ref[rows], axis=-1, keepdims=True), (tq, tk))
        sweep(sum_blocks)
        row_sums()

    for pair in range(DSA_HEADS // 2):
        out = jnp.zeros((tq, LANES), F32)
        for h in (2 * pair, 2 * pair + 1):
            rows = slice(h * tq, (h + 1) * tq)
            o_lat = (acc_ref[rows] / lsum_ref[rows]).astype(BF16)
            out = out + jnp.dot(o_lat, wuv_ref[h], preferred_element_type=F32)
        o_ref[0, :, pair * LANES:(pair + 1) * LANES] = out.astype(BF16)


def _dsa(big, w_idx_t, c, w_uk, w_uv, bias_tiles, t5_table, q_col, qi_col, k2_col):
    b, s, _ = big.shape
    assert s % DSA_TK == 0 and DSA_TK % DSA_TQ == 0, s
    k_top = min(TOPK_MAX, s // 4)
    hq = DSA_HEADS * DSA_DH
    hi = IDX_HEADS * IDX_DIM
    mine = (np.arange(LANES)[None, :] // DSA_DH) == (np.arange(DSA_HEADS)[:, None] % 2)
    wuk = jnp.where(mine[:, :, None], jnp.concatenate([w_uk, w_uk], axis=1), 0.0).astype(BF16)
    wuv = jnp.where(mine[:, None, :], jnp.concatenate([w_uv, w_uv], axis=2), 0.0).astype(BF16)
    nq = s // DSA_TQ
    rows = DSA_HEADS * DSA_TQ
    return pl.pallas_call(
        functools.partial(_dsa_kernel, k_top=k_top),
        grid=(b, nq),
        in_specs=[
            pl.BlockSpec((1, DSA_TQ, hq), lambda bi, qi: (bi, qi, q_col)),
            pl.BlockSpec((1, DSA_TQ, hi), lambda bi, qi: (bi, qi, qi_col)),
            pl.BlockSpec((1, IDX_HEADS, DSA_TQ), lambda bi, qi: (bi, 0, qi)),
            pl.BlockSpec((1, s, LANES), lambda bi, qi: (bi, 0, k2_col)),
            pl.BlockSpec((1, s, DSA_LATENT), lambda bi, qi: (bi, 0, 0)),
            _resident(wuk.shape), _resident(wuv.shape), _resident(bias_tiles.shape), _resident(t5_table.shape),
        ],
        out_specs=pl.BlockSpec((1, DSA_TQ, hq), lambda bi, qi: (bi, qi, 0)),
        out_shape=jax.ShapeDtypeStruct((b, s, hq), BF16),
        scratch_shapes=[
            pltpu.VMEM((s // DSA_TK + COUNT_STEP - 1, DSA_TK, DSA_TQ), jnp.int32),
            pltpu.VMEM((s // DSA_TK + 1, DSA_TQ, DSA_TK), F32),
            pltpu.VMEM((IDX_HEADS, LANES, DSA_TQ), BF16),
            pltpu.VMEM((rows, DSA_LATENT), BF16),
            pltpu.VMEM((rows, FAR_STEP * DSA_TK), BF16),
            pltpu.VMEM((rows, DSA_TK), F32),
            pltpu.VMEM((rows, DSA_TK), F32),
            pltpu.VMEM((rows, DSA_LATENT), F32),
            pltpu.VMEM((rows, 1), F32),
            pltpu.VMEM((8, LANES), F32),
        ],
        compiler_params=_params("parallel", "arbitrary"),
        name="dsa",
    )(big, big, w_idx_t, big, c, wuk, wuv, bias_tiles, t5_table)


def _even_mixer(x2, b, s, g, w_in, w_gate, b_gate, norm_g, fox_b):
    hk = GLA_HEADS * GLA_DK
    hv = GLA_HEADS * GLA_DV
    hf = FOX_HEADS * FOX_DH
    o = np.cumsum([0, hk, hk, hv, hv, GLA_GATE_RANK, hf, hf, hf, FOX_HEADS])
    col = lambda n: w_in[:, o[n]:o[n + 1]]
    w_big = jnp.concatenate([col(0), col(1), col(2), col(3), col(5), col(6), col(7)], axis=1)
    w_small = jnp.pad(jnp.concatenate([col(4), col(8)], axis=1), ((0, 0), (0, LANES - GLA_GATE_RANK - FOX_HEADS)))
    big, small = _inproj(x2, g, w_big, w_small)
    big = big.reshape(b, s, -1)
    small = small.reshape(b, s, LANES)
    o_gla = _gla(big, small, w_gate, b_gate, norm_g)
    f_logit_t = small[:, :, GLA_GATE_RANK:GLA_GATE_RANK + FOX_HEADS].transpose(0, 2, 1)
    f_cum = _fox_gate(f_logit_t, fox_b)
    fox_col = (2 * hk + 2 * hv) // hf
    o_fox = _fox(big, f_cum, fox_col, fox_col + 1, fox_col + 2)
    return [o_gla.reshape(b * s, hv), o_fox.reshape(b * s, hf)]


def _odd_mixer(x2, b, s, g, w_in, kv_g, w_uk, w_uv, bias_tiles, t5_table):
    hq = DSA_HEADS * DSA_DH
    hi = IDX_HEADS * IDX_DIM
    o = np.cumsum([0, hq, DSA_LATENT, hi, IDX_DIM, IDX_HEADS])
    col = lambda n: w_in[:, o[n]:o[n + 1]]
    w_big = jnp.concatenate([col(0), col(2), col(3), col(3)], axis=1)
    w_small = jnp.pad(col(4), ((0, 0), (0, LANES - IDX_HEADS)))
    big, small, c = _inproj(x2, g, w_big, w_small, col(1), kv_g)
    big = big.reshape(b, s, -1)
    w_idx_t = small.reshape(b, s, LANES)[:, :, :IDX_HEADS].transpose(0, 2, 1)
    o_dsa = _dsa(big, w_idx_t, c.reshape(b, s, DSA_LATENT), w_uk, w_uv, bias_tiles, t5_table,
                 0, hq // hi, (hq + hi) // LANES)
    return [o_dsa.reshape(b * s, hq)]


def kernel(x, norm_g, ffn_w_in, ffn_w_out, even_w_in, gla_w_gate, gla_b_gate, gla_norm_g, fox_b_f, even_w_out,
           odd_w_in, mla_kv_norm_g, mla_w_uk, mla_w_uv, odd_w_out, t5_table, final_norm_g):
    b, s, d = x.shape
    depth = norm_g.shape[0]
    x2 = x.reshape(b * s, d)
    bias_tiles = _bias_tiles(t5_table) if depth > 1 else None
    for layer in range(depth):
        g = norm_g[layer]
        j = layer // 2
        x2 = _ffn(x2, g[0], ffn_w_in[layer, 0], ffn_w_out[layer, 0])
        if layer % 2 == 0:
            heads = _even_mixer(x2, b, s, g[1], even_w_in[j], gla_w_gate[j], gla_b_gate[j], gla_norm_g[j], fox_b_f[j])
            w_proj = even_w_out[j]
        else:
            heads = _odd_mixer(x2, b, s, g[1], odd_w_in[j], mla_kv_norm_g[j], mla_w_uk[j], mla_w_uv[j],
                               bias_tiles, t5_table)
            w_proj = odd_w_out[j]
        last = layer == depth - 1
        x2 = _ffn(x2, g[2], ffn_w_in[layer, 1], ffn_w_out[layer, 1], final_norm_g if last else None,
                  mixer=(heads, w_proj))
    return x2.reshape(b, s, d)
```
